```python
import math
import jax, jax.numpy as jnp
from jax import lax
import numpy as np

D_MODEL = 1024
BATCH = 32
SEQ = 2048
DEPTH = 4

N_MIXERS = 3
BLOCK_Q = 128
EPS = 1e-6

GLA_HEADS = 4
GLA_KEY_DIM = D_MODEL // 2
GLA_VAL_DIM = D_MODEL
GLA_DK = GLA_KEY_DIM // GLA_HEADS
GLA_DV = GLA_VAL_DIM // GLA_HEADS
GLA_GATE_RANK = 16
GLA_GATE_NORMALIZER = 16.0
GLA_CHUNK = 64
GLA_IN = 2 * GLA_KEY_DIM + 2 * GLA_VAL_DIM + GLA_GATE_RANK

MLA_HEADS = 16
MLA_Q_RANK = 256
MLA_KV_RANK = 128
MLA_NOPE = 128
MLA_ROPE = 64
MLA_V = 128
MLA_WIDTH = MLA_HEADS * MLA_V
MLA_IN = MLA_Q_RANK + MLA_KV_RANK + MLA_ROPE + MLA_WIDTH
ROPE_THETA = 10000.0

SB_HEADS = 16
SB_HEAD_DIM = 64
SB_WIDTH = SB_HEADS * SB_HEAD_DIM
SB_IN = 4 * SB_WIDTH

N_GLA_LAYERS = (DEPTH + 2) // 3
N_MLA_LAYERS = (DEPTH + 1) // 3
N_SB_LAYERS = DEPTH // 3

kernel_name = "interleaved_gla_mla_stickbreaking_trunk"


def rmsnorm(x, g):
    xf = x.astype(jnp.float32)
    y = xf * lax.rsqrt(jnp.mean(xf * xf, axis=-1, keepdims=True) + EPS)
    return (y * g.astype(jnp.float32)).astype(x.dtype)


def rope_tables(seq, dim):
    pos = jnp.arange(seq, dtype=jnp.float32)
    inv = ROPE_THETA ** (-jnp.arange(0, dim, 2, dtype=jnp.float32) / dim)
    ang = pos[:, None] * inv[None, :]
    return jnp.cos(ang), jnp.sin(ang)


def apply_rope(x, cos, sin):
    x1, x2 = jnp.split(x, 2, axis=-1)
    c = cos.astype(x.dtype)
    s = sin.astype(x.dtype)
    return jnp.concatenate([x1 * c - x2 * s, x1 * s + x2 * c], axis=-1)


def gla_mixer(h, w_in, w_gate2, b_gate, norm_g, w_out):
    B, S, _ = h.shape
    proj = h @ w_in
    K, V = GLA_KEY_DIM, GLA_VAL_DIM
    q, k, v, g, a_low = jnp.split(proj, [K, 2 * K, 2 * K + V, 2 * K + 2 * V], axis=-1)
    log_a = jax.nn.log_sigmoid((a_low @ w_gate2 + b_gate).astype(jnp.float32)) / GLA_GATE_NORMALIZER
    q = q * (GLA_DK ** -0.5)
    n_chunks = S // GLA_CHUNK

    def to_chunks(t, d):
        return t.astype(jnp.float32).reshape(B, n_chunks, GLA_CHUNK, GLA_HEADS, d).transpose(1, 0, 3, 2, 4)

    qc, kc, vc = to_chunks(q, GLA_DK), to_chunks(k, GLA_DK), to_chunks(v, GLA_DV)
    lac = to_chunks(log_a, GLA_DK)
    mask = jnp.tril(jnp.ones((GLA_CHUNK, GLA_CHUNK), dtype=bool))

    def step(state, inp):
        qb, kb, vb, la = inp
        b = jnp.cumsum(la, axis=2)
        b_last = b[:, :, -1, :]
        o_inter = jnp.einsum('bhtd,bhde->bhte', qb * jnp.exp(b), state)
        diff = b[:, :, :, None, :] - b[:, :, None, :, :]
        decay = jnp.exp(jnp.where(mask[:, :, None], diff, -jnp.inf))
        attn = jnp.einsum('bhtd,bhsd,bhtsd->bhts', qb, kb, decay)
        o_intra = jnp.einsum('bhts,bhse->bhte', attn, vb)
        k_dec = kb * jnp.exp(b_last[:, :, None, :] - b)
        new_state = jnp.exp(b_last)[..., None] * state + jnp.einsum('bhsd,bhse->bhde', k_dec, vb)
        return new_state, o_inter + o_intra

    state0 = jnp.zeros((B, GLA_HEADS, GLA_DK, GLA_DV), jnp.float32)
    _, o = lax.scan(step, state0, (qc, kc, vc, lac))
    o = o.transpose(1, 0, 3, 2, 4).reshape(B, S, GLA_HEADS, GLA_DV).astype(h.dtype)
    o = rmsnorm(o, norm_g).reshape(B, S, GLA_VAL_DIM)
    return (o * jax.nn.silu(g)) @ w_out


def mla_mixer(h, w_in, q_norm_g, kv_norm_g, w_uq, w_ukv, w_out, cos, sin):
    B, S, _ = h.shape
    proj = h @ w_in
    c_q, c_kv, k_rope, g = jnp.split(
        proj, [MLA_Q_RANK, MLA_Q_RANK + MLA_KV_RANK, MLA_Q_RANK + MLA_KV_RANK + MLA_ROPE], axis=-1)
    q = (rmsnorm(c_q, q_norm_g) @ w_uq).reshape(B, S, MLA_HEADS, MLA_NOPE + MLA_ROPE)
    q_nope, q_rope = jnp.split(q, [MLA_NOPE], axis=-1)
    q_rope = apply_rope(q_rope, cos[:, None, :], sin[:, None, :])
    kv = (rmsnorm(c_kv, kv_norm_g) @ w_ukv).reshape(B, S, MLA_HEADS, MLA_NOPE + MLA_V)
    k_nope, v = jnp.split(kv, [MLA_NOPE], axis=-1)
    k_rope = apply_rope(k_rope, cos, sin)
    scale = (MLA_NOPE + MLA_ROPE) ** -0.5
    outs = []
    for i in range(S // BLOCK_Q):
        q0, k_end = i * BLOCK_Q, (i + 1) * BLOCK_Q
        s = (jnp.einsum('bqhd,bkhd->bhqk', q_nope[:, q0:k_end], k_nope[:, :k_end])
             + jnp.einsum('bqhr,bkr->bhqk', q_rope[:, q0:k_end], k_rope[:, :k_end]))
        s = s.astype(jnp.float32) * scale
        causal = (q0 + jnp.arange(BLOCK_Q))[:, None] >= jnp.arange(k_end)[None, :]
        p = jax.nn.softmax(jnp.where(causal, s, -jnp.inf), axis=-1).astype(v.dtype)
        outs.append(jnp.einsum('bhqk,bkhd->bqhd', p, v[:, :k_end]))
    o = jnp.concatenate(outs, axis=1).reshape(B, S, MLA_WIDTH)
    return (o * jax.nn.silu(g)) @ w_out


def sb_mixer(h, w_in, w_out):
    B, S, _ = h.shape
    proj = h @ w_in
    q, k, v, g = jnp.split(proj, 4, axis=-1)
    q = q.reshape(B, S, SB_HEADS, SB_HEAD_DIM)
    k = k.reshape(B, S, SB_HEADS, SB_HEAD_DIM)
    v = v.reshape(B, S, SB_HEADS, SB_HEAD_DIM)
    scale = SB_HEAD_DIM ** -0.5
    outs = []
    for i in range(S // BLOCK_Q):
        q0, k_end = i * BLOCK_Q, (i + 1) * BLOCK_Q
        z = jnp.einsum('bqhd,bkhd->bhqk', q[:, q0:k_end], k[:, :k_end]).astype(jnp.float32) * scale
        strict = (q0 + jnp.arange(BLOCK_Q))[:, None] > jnp.arange(k_end)[None, :]
        log_beta = jax.nn.log_sigmoid(z)
        log_1m_beta = jnp.where(strict, jax.nn.log_sigmoid(-z), 0.0)
        after = lax.cumsum(log_1m_beta, axis=3, reverse=True) - log_1m_beta
        a = jnp.where(strict, jnp.exp(log_beta + after), 0.0).astype(v.dtype)
        outs.append(jnp.einsum('bhqk,bkhd->bqhd', a, v[:, :k_end]))
    o = jnp.concatenate(outs, axis=1).reshape(B, S, SB_WIDTH)
    return (o * jax.nn.silu(g)) @ w_out


def setup_inputs(seed: int = 0) -> dict:
    key = jax.random.key(seed)
    ks = jax.random.split(key, 20)

    def w(k, shape, fan_in):
        return jax.random.normal(k, shape, jnp.float32) * (fan_in ** -0.5)

    def gain(k, shape):
        return 1.0 + 0.01 * jax.random.normal(k, shape, jnp.float32)

    nA, nB, nC = N_GLA_LAYERS, N_MLA_LAYERS, N_SB_LAYERS
    return {
        "x": jax.random.normal(ks[0], (BATCH, SEQ, D_MODEL), jnp.float32),
        "norm_g": gain(ks[1], (DEPTH, D_MODEL)),
        "final_g": gain(ks[2], (D_MODEL,)),
        "gla_w_in": w(ks[3], (nA, D_MODEL, GLA_IN), D_MODEL),
        "gla_w_gate2": w(ks[4], (nA, GLA_GATE_RANK, GLA_KEY_DIM), GLA_GATE_RANK),
        "gla_b_gate": 0.1 * jax.random.normal(ks[5], (nA, GLA_KEY_DIM), jnp.float32),
        "gla_norm_g": gain(ks[6], (nA, GLA_DV)),
        "gla_w_out": w(ks[7], (nA, GLA_VAL_DIM, D_MODEL), GLA_VAL_DIM),
        "mla_w_in": w(ks[8], (nB, D_MODEL, MLA_IN), D_MODEL),
        "mla_q_norm_g": gain(ks[9], (nB, MLA_Q_RANK)),
        "mla_kv_norm_g": gain(ks[10], (nB, MLA_KV_RANK)),
        "mla_w_uq": w(ks[11], (nB, MLA_Q_RANK, MLA_HEADS * (MLA_NOPE + MLA_ROPE)), MLA_Q_RANK),
        "mla_w_ukv": w(ks[12], (nB, MLA_KV_RANK, MLA_HEADS * (MLA_NOPE + MLA_V)), MLA_KV_RANK),
        "mla_w_out": w(ks[13], (nB, MLA_WIDTH, D_MODEL), MLA_WIDTH),
        "sb_w_in": w(ks[14], (nC, D_MODEL, SB_IN), D_MODEL),
        "sb_w_out": w(ks[15], (nC, SB_WIDTH, D_MODEL), SB_WIDTH),
    }


def reference(x, norm_g, final_g, gla_w_in, gla_w_gate2, gla_b_gate, gla_norm_g, gla_w_out,
              mla_w_in, mla_q_norm_g, mla_kv_norm_g, mla_w_uq, mla_w_ukv, mla_w_out,
              sb_w_in, sb_w_out):
    cos, sin = rope_tables(x.shape[1], MLA_ROPE)
    for layer in range(DEPTH):
        kind, j = layer % N_MIXERS, layer // N_MIXERS
        h = rmsnorm(x, norm_g[layer])
        if kind == 0:
            y = gla_mixer(h, gla_w_in[j], gla_w_gate2[j], gla_b_gate[j], gla_norm_g[j], gla_w_out[j])
        elif kind == 1:
            y = mla_mixer(h, mla_w_in[j], mla_q_norm_g[j], mla_kv_norm_g[j], mla_w_uq[j],
                          mla_w_ukv[j], mla_w_out[j], cos, sin)
        else:
            y = sb_mixer(h, sb_w_in[j], sb_w_out[j])
        x = x + y
    return rmsnorm(x, final_g)
```

```python
import functools
import math

import jax
import jax.numpy as jnp
from jax import lax
from jax.experimental import pallas as pl
from jax.experimental.pallas import tpu as pltpu

EPS = 1e-6
F32 = jnp.float32
BF16 = jnp.bfloat16

LANES = 128
VMEM_LIMIT_BYTES = 56 * 1024 * 1024

GLA_HEADS = 4
GLA_CHUNK = 64
GLA_GATE_RANK = 16
GLA_GATE_NORMALIZER = 16.0
MLA_HEADS = 16
MLA_Q_RANK = 256
MLA_KV_RANK = 128
MLA_NOPE = 128
MLA_ROPE = 64
MLA_V = 128
ROPE_THETA = 10000.0
SB_HEADS = 16
SB_HEAD_DIM = 64

ROW_TILE = 256
MLA_TILE = 256
SB_TILE = 128


def _dot(a, b):
    return jnp.dot(a, b, preferred_element_type=F32)


def _dot_nt(a, b):
    return lax.dot_general(a, b, (((1,), (1,)), ((), ())), preferred_element_type=F32)


def _dot_tn(a, b):
    return lax.dot_general(a, b, (((0,), (0,)), ((), ())), preferred_element_type=F32)


def _rms(x, g):
    return x * lax.rsqrt(jnp.mean(x * x, axis=-1, keepdims=True) + EPS) * g


def _log_sigmoid(z):
    return jnp.minimum(z, 0.0) - jnp.log1p(jnp.exp(-jnp.abs(z)))


def _silu(g):
    return g * (1.0 / (1.0 + jnp.exp(-g)))


def _params(*sem):
    return pltpu.CompilerParams(dimension_semantics=sem, vmem_limit_bytes=VMEM_LIMIT_BYTES)


def _full(shape):
    return pl.BlockSpec(shape, lambda *_: (0,) * len(shape))


def _gla_in_body(x_ref, ng_ref, wq_ref, wk_ref, wv_ref, wg_ref, wa_ref, w2_ref, bg_ref, tri_ref,
                 qe_ref, kn_ref, kd_ref, v_ref, g_ref, dl_ref, *, q_scale):
    h = _rms(x_ref[...], ng_ref[...]).astype(BF16)
    q = _dot(h, wq_ref[...]) * q_scale
    k = _dot(h, wk_ref[...])
    v_ref[...] = _dot(h, wv_ref[...]).astype(BF16)
    g_ref[...] = _dot(h, wg_ref[...]).astype(BF16)
    a_low = _dot(h, wa_ref[...]).astype(BF16)
    log_a = _log_sigmoid(_dot(a_low, w2_ref[...]) + bg_ref[...]) * (1.0 / GLA_GATE_NORMALIZER)
    hi = log_a.astype(BF16)
    lo = (log_a - hi.astype(F32)).astype(BF16)
    tri = tri_ref[...]
    b = _dot(tri, hi) + _dot(tri, lo)
    tm, dk = b.shape
    b3 = b.reshape(tm // GLA_CHUNK, GLA_CHUNK, dk)
    b_last = b3[:, GLA_CHUNK - 1:GLA_CHUNK, :]
    qe_ref[...] = (q * jnp.exp(b)).astype(BF16)
    kn_ref[...] = (k * jnp.exp(-b)).astype(BF16)
    k3 = k.reshape(tm // GLA_CHUNK, GLA_CHUNK, dk)
    kd_ref[...] = (k3 * jnp.exp(b_last - b3)).reshape(tm, dk).astype(BF16)
    dl_ref[0] = jnp.exp(b_last).reshape(tm // GLA_CHUNK, dk)


def _gla_in(x2, ng, w_in, w_gate2, b_gate):
    t, d = x2.shape
    key_dim = w_gate2.shape[1]
    val_dim = (w_in.shape[1] - 2 * key_dim - GLA_GATE_RANK) // 2
    dk = key_dim // GLA_HEADS
    tm = ROW_TILE
    wq = w_in[:, :key_dim].astype(BF16)
    wk = w_in[:, key_dim:2 * key_dim].astype(BF16)
    wv = w_in[:, 2 * key_dim:2 * key_dim + val_dim].astype(BF16)
    wg = w_in[:, 2 * key_dim + val_dim:2 * key_dim + 2 * val_dim].astype(BF16)
    wa = jnp.pad(w_in[:, 2 * key_dim + 2 * val_dim:], ((0, 0), (0, LANES - GLA_GATE_RANK))).astype(BF16)
    w2 = jnp.pad(w_gate2, ((0, LANES - GLA_GATE_RANK), (0, 0))).astype(BF16)
    r = jnp.arange(tm)
    tri = ((r[:, None] // GLA_CHUNK == r[None, :] // GLA_CHUNK) & (r[:, None] >= r[None, :])).astype(BF16)
    row = lambda n: pl.BlockSpec((tm, n), lambda i: (i, 0))
    outs = pl.pallas_call(
        functools.partial(_gla_in_body, q_scale=dk ** -0.5),
        grid=(t // tm,),
        in_specs=[row(d), _full((1, d)), _full(wq.shape), _full(wk.shape), _full(wv.shape), _full(wg.shape),
                  _full(wa.shape), _full(w2.shape), _full((1, key_dim)), _full((tm, tm))],
        out_specs=[row(key_dim), row(key_dim), row(key_dim), row(val_dim), row(val_dim),
                   pl.BlockSpec((1, tm // GLA_CHUNK, key_dim), lambda i: (i, 0, 0))],
        out_shape=[jax.ShapeDtypeStruct((t, key_dim), BF16)] * 3 + [jax.ShapeDtypeStruct((t, val_dim), BF16)] * 2
        + [jax.ShapeDtypeStruct((t // tm, tm // GLA_CHUNK, key_dim), F32)],
        compiler_params=_params("parallel"),
        name="gla_in",
    )(x2, ng.reshape(1, d), wq, wk, wv, wg, wa, w2, b_gate.reshape(1, key_dim), tri)
    return outs


def _gla_mix_body(qe_ref, kn_ref, kd_ref, v_ref, g_ref, dl_ref, gn_ref, o_ref, oi_ref, ut_ref):
    n_chunks = dl_ref.shape[1]
    c = GLA_CHUNK
    ri = lax.broadcasted_iota(jnp.int32, (c, c), 0)
    ci = lax.broadcasted_iota(jnp.int32, (c, c), 1)
    causal = ri >= ci

    def intra(i, carry):
        r = pl.ds(pl.multiple_of(i * c, c), c)
        v = v_ref[0, r, :]
        attn = jnp.where(causal, _dot_nt(qe_ref[0, r, :], kn_ref[0, r, :]), 0.0).astype(BF16)
        oi_ref[r, :] = _dot(attn, v)
        ut_ref[i] = _dot_tn(v, kd_ref[0, r, :])
        return carry

    lax.fori_loop(0, n_chunks, intra, 0)

    gn = gn_ref[...]

    def inter(i, state_t):
        r = pl.ds(pl.multiple_of(i * c, c), c)
        o = oi_ref[r, :] + _dot_nt(qe_ref[0, r, :], state_t.astype(BF16))
        y = _rms(o, gn) * _silu(g_ref[0, r, :].astype(F32))
        o_ref[0, r, :] = y.astype(BF16)
        return dl_ref[0, pl.ds(i, 1), :] * state_t + ut_ref[i]

    dv, dk = ut_ref.shape[1], ut_ref.shape[2]
    lax.fori_loop(0, n_chunks, inter, jnp.zeros((dv, dk), F32))


def _gla_mix(qe, kn, kd, v, g, dlast, gn):
    b, s, key_dim = qe.shape
    val_dim = v.shape[2]
    dk, dv = key_dim // GLA_HEADS, val_dim // GLA_HEADS
    n_chunks = s // GLA_CHUNK
    kspec = pl.BlockSpec((1, s, dk), lambda i, j: (i, 0, j))
    vspec = pl.BlockSpec((1, s, dv), lambda i, j: (i, 0, j))
    return pl.pallas_call(
        _gla_mix_body,
        grid=(b, GLA_HEADS),
        in_specs=[kspec, kspec, kspec, vspec, vspec,
                  pl.BlockSpec((1, n_chunks, dk), lambda i, j: (i, 0, j)), _full((1, dv))],
        out_specs=vspec,
        out_shape=jax.ShapeDtypeStruct((b, s, val_dim), BF16),
        scratch_shapes=[pltpu.VMEM((s, dv), F32), pltpu.VMEM((n_chunks, dv, dk), F32)],
        compiler_params=_params("parallel", "parallel"),
        name="gla_mix",
    )(qe, kn, kd, v, g, dlast, gn.reshape(1, dv))


def _rope(x, cos, sin_signed):
    return x * cos + pltpu.roll(x, 64, 1) * sin_signed


def _mla_in_body(x_ref, ng_ref, wa_ref, wg_ref, gq_ref, gkv_ref, wqn_ref, wqr_ref, wkn_ref, wv_ref,
                 cos_ref, sin_ref, qn_ref, qr_ref, kn_ref, kr_ref, v_ref, g_ref, *, scale):
    h = _rms(x_ref[...], ng_ref[...]).astype(BF16)
    pa = _dot(h, wa_ref[...])
    g_ref[...] = _dot(h, wg_ref[...]).astype(BF16)
    cos, sin = cos_ref[...], sin_ref[...]
    c_q = _rms(pa[:, :MLA_Q_RANK], gq_ref[...]).astype(BF16)
    c_kv = _rms(pa[:, MLA_Q_RANK:MLA_Q_RANK + MLA_KV_RANK], gkv_ref[...]).astype(BF16)
    kr_ref[...] = _rope(pa[:, MLA_Q_RANK + MLA_KV_RANK:], cos, sin).astype(BF16)
    qn_ref[...] = (_dot(c_q, wqn_ref[...]) * scale).astype(BF16)
    qr = _dot(c_q, wqr_ref[...])
    for p in range(qr.shape[1] // LANES):
        sl = slice(p * LANES, (p + 1) * LANES)
        qr_ref[:, sl] = (_rope(qr[:, sl], cos, sin) * scale).astype(BF16)
    kn_ref[...] = _dot(c_kv, wkn_ref[...]).astype(BF16)
    v_ref[...] = _dot(c_kv, wv_ref[...]).astype(BF16)


def _mla_in(x2, ng, w_in, gq, gkv, w_uq, w_ukv, seq):
    t, d = x2.shape
    tm = ROW_TILE
    hq, half = MLA_HEADS, MLA_ROPE // 2
    o_kv, o_kr, o_g = MLA_Q_RANK, MLA_Q_RANK + MLA_KV_RANK, MLA_Q_RANK + MLA_KV_RANK + MLA_ROPE
    kr1, kr2 = w_in[:, o_kr:o_kr + half], w_in[:, o_kr + half:o_g]
    wa = jnp.concatenate([w_in[:, :o_kr], kr1, kr1, kr2, kr2], axis=1).astype(BF16)
    wg = w_in[:, o_g:].astype(BF16)
    uq = w_uq.reshape(MLA_Q_RANK, hq, MLA_NOPE + MLA_ROPE)
    wqn = uq[:, :, :MLA_NOPE].reshape(MLA_Q_RANK, hq * MLA_NOPE).astype(BF16)
    x1 = uq[:, :, MLA_NOPE:MLA_NOPE + half].reshape(MLA_Q_RANK, hq // 2, 2 * half)
    x2r = uq[:, :, MLA_NOPE + half:].reshape(MLA_Q_RANK, hq // 2, 2 * half)
    wqr = jnp.concatenate([x1, x2r], axis=2).reshape(MLA_Q_RANK, hq * MLA_ROPE).astype(BF16)
    ukv = w_ukv.reshape(MLA_KV_RANK, hq, MLA_NOPE + MLA_V)
    wkn = ukv[:, :, :MLA_NOPE].reshape(MLA_KV_RANK, hq * MLA_NOPE).astype(BF16)
    wv = ukv[:, :, MLA_NOPE:].reshape(MLA_KV_RANK, hq * MLA_V).astype(BF16)

    pos = jnp.arange(seq, dtype=F32)
    inv = ROPE_THETA ** (-jnp.arange(0, MLA_ROPE, 2, dtype=F32) / MLA_ROPE)
    ang = pos[:, None] * inv[None, :]
    cos, sin = jnp.cos(ang), jnp.sin(ang)
    cos4 = jnp.concatenate([cos, cos, cos, cos], axis=1)
    sin4 = jnp.concatenate([-sin, -sin, sin, sin], axis=1)

    scale = (MLA_NOPE + MLA_ROPE) ** -0.5
    row = lambda n: pl.BlockSpec((tm, n), lambda i: (i, 0))
    tab = pl.BlockSpec((tm, LANES), lambda i: (i % (seq // tm), 0))
    wide = hq * MLA_NOPE
    return pl.pallas_call(
        functools.partial(_mla_in_body, scale=scale),
        grid=(t // tm,),
        in_specs=[row(d), _full((1, d)), _full(wa.shape), _full(wg.shape), _full((1, MLA_Q_RANK)),
                  _full((1, MLA_KV_RANK)), _full(wqn.shape), _full(wqr.shape), _full(wkn.shape),
                  _full(wv.shape), tab, tab],
        out_specs=[row(wide), row(hq * MLA_ROPE), row(wide), row(LANES), row(wide), row(wide)],
        out_shape=[jax.ShapeDtypeStruct((t, wide), BF16), jax.ShapeDtypeStruct((t, hq * MLA_ROPE), BF16),
                   jax.ShapeDtypeStruct((t, wide), BF16), jax.ShapeDtypeStruct((t, LANES), BF16),
                   jax.ShapeDtypeStruct((t, wide), BF16), jax.ShapeDtypeStruct((t, wide), BF16)],
        compiler_params=_params("parallel"),
        name="mla_in",
    )(x2, ng.reshape(1, d), wa, wg, gq.reshape(1, -1), gkv.reshape(1, -1), wqn, wqr, wkn, wv, cos4, sin4)


def _mla_attn_body(qn_ref, qr_ref, kn_ref, kr_ref, v_ref, g_ref, o_ref, qc_ref, kc_ref):
    s = qn_ref.shape[1]
    t = MLA_TILE
    n_tiles = s // t
    lane = lax.broadcasted_iota(jnp.int32, (s, LANES), 1)
    mine = ((lane // (MLA_ROPE // 2)) % 2) == (pl.program_id(1) % 2)
    qc_ref[:, :LANES] = qn_ref[0]
    qc_ref[:, LANES:] = jnp.where(mine, qr_ref[0], jnp.zeros((), BF16))
    kc_ref[:, :LANES] = kn_ref[0]
    kc_ref[:, LANES:] = kr_ref[0]
    ri = lax.broadcasted_iota(jnp.int32, (t, t), 0)
    ci = lax.broadcasted_iota(jnp.int32, (t, t), 1)
    causal = ri >= ci

    def q_tile(i, carry):
        rq = pl.ds(pl.multiple_of(i * t, t), t)
        q = qc_ref[rq, :]

        def update(scores, v, m, l, acc):
            m_new = jnp.maximum(m, jnp.max(scores, axis=-1, keepdims=True))
            p = jnp.exp(scores - m_new)
            alpha = jnp.exp(m - m_new)
            l = alpha * l + jnp.sum(p, axis=-1, keepdims=True)
            acc = alpha * acc + _dot(p.astype(BF16), v)
            return m_new, l, acc

        def k_tile(j, st):
            rk = pl.ds(pl.multiple_of(j * t, t), t)
            return update(_dot_nt(q, kc_ref[rk, :]), v_ref[0, rk, :], *st)

        init = (jnp.full((t, 1), -jnp.inf, F32), jnp.zeros((t, 1), F32), jnp.zeros((t, MLA_V), F32))
        st = lax.fori_loop(0, i, k_tile, init)
        diag = jnp.where(causal, _dot_nt(q, kc_ref[rq, :]), -jnp.inf)
        _, l, acc = update(diag, v_ref[0, rq, :], *st)
        o = acc / l
        o_ref[0, rq, :] = (o * _silu(g_ref[0, rq, :].astype(F32))).astype(BF16)
        return carry

    lax.fori_loop(0, n_tiles, q_tile, 0)


def _mla_attn(qn, qr, kn, kr, v, g):
    b, s, wide = qn.shape
    spec = pl.BlockSpec((1, s, LANES), lambda i, j: (i, 0, j))
    return pl.pallas_call(
        _mla_attn_body,
        grid=(b, MLA_HEADS),
        in_specs=[spec, pl.BlockSpec((1, s, LANES), lambda i, j: (i, 0, j // 2)), spec,
                  pl.BlockSpec((1, s, LANES), lambda i, j: (i, 0, 0)), spec, spec],
        out_specs=spec,
        out_shape=jax.ShapeDtypeStruct((b, s, wide), BF16),
        scratch_shapes=[pltpu.VMEM((s, 2 * LANES), BF16), pltpu.VMEM((s, 2 * LANES), BF16)],
        compiler_params=_params("parallel", "parallel"),
        name="mla_attn",
    )(qn, qr, kn, kr, v, g)


def _sb_in_body(x_ref, ng_ref, wq_ref, wk_ref, wv_ref, wg_ref, q_ref, k_ref, v_ref, g_ref, *, scale):
    h = _rms(x_ref[...], ng_ref[...]).astype(BF16)
    q_ref[...] = (_dot(h, wq_ref[...]) * scale).astype(BF16)
    k_ref[...] = _dot(h, wk_ref[...]).astype(BF16)
    v_ref[...] = _dot(h, wv_ref[...]).astype(BF16)
    g_ref[...] = _dot(h, wg_ref[...]).astype(BF16)


def _sb_in(x2, ng, w_in):
    t, d = x2.shape
    width = w_in.shape[1] // 4
    tm = ROW_TILE
    ws = [w_in[:, i * width:(i + 1) * width].astype(BF16) for i in range(4)]
    row = lambda n: pl.BlockSpec((tm, n), lambda i: (i, 0))
    return pl.pallas_call(
        functools.partial(_sb_in_body, scale=SB_HEAD_DIM ** -0.5),
        grid=(t // tm,),
        in_specs=[row(d), _full((1, d))] + [_full(w.shape) for w in ws],
        out_specs=[row(width)] * 4,
        out_shape=[jax.ShapeDtypeStruct((t, width), BF16)] * 4,
        compiler_params=_params("parallel"),
        name="sb_in",
    )(x2, ng.reshape(1, d), *ws)


def _sb_attn_body(q_ref, k_ref, v_ref, g_ref, m2_ref, o_ref):
    s = q_ref.shape[1]
    t = SB_TILE
    n_tiles = s // t
    ri = lax.broadcasted_iota(jnp.int32, (t, t), 0)
    ci = lax.broadcasted_iota(jnp.int32, (t, t), 1)
    strict = ri > ci
    lane = lax.broadcasted_iota(jnp.int32, (t, LANES), 1)
    first = lane < SB_HEAD_DIM
    m2 = m2_ref[...]

    def block(z, v, carry, acc, masked):
        log_beta = _log_sigmoid(z)
        log_1m = log_beta - z
        if masked:
            log_1m = jnp.where(strict, log_1m, 0.0)
        hi = log_1m.astype(BF16)
        lo = (log_1m - hi.astype(F32)).astype(BF16)
        res = _dot(jnp.concatenate([hi, lo], axis=1), m2)
        a = jnp.exp(log_beta + res[:, :t] + carry)
        if masked:
            a = jnp.where(strict, a, 0.0)
        return carry + res[:, t:], acc + _dot(a.astype(BF16), v)

    def q_tile(i, c0):
        rq = pl.ds(pl.multiple_of(i * t, t), t)
        q = q_ref[0, rq, :]
        zero = jnp.zeros((), BF16)
        qs = (jnp.where(first, q, zero), jnp.where(first, zero, q))

        def k_tile(n, st):
            j = i - 1 - n
            rk = pl.ds(pl.multiple_of(j * t, t), t)
            k, v = k_ref[0, rk, :], v_ref[0, rk, :]
            return tuple(block(_dot_nt(qh, k), v, c, a, False) for qh, (c, a) in zip(qs, st))

        k, v = k_ref[0, rq, :], v_ref[0, rq, :]
        z0 = jnp.zeros((t, t), F32)
        st = tuple(block(_dot_nt(qh, k), v, z0, jnp.zeros((t, LANES), F32), True) for qh in qs)
        st = lax.fori_loop(0, i, k_tile, st)
        o = jnp.where(first, st[0][1], st[1][1])
        o_ref[0, rq, :] = (o * _silu(g_ref[0, rq, :].astype(F32))).astype(BF16)
        return c0

    lax.fori_loop(0, n_tiles, q_tile, 0)


def _sb_attn(q, k, v, g):
    b, s, width = q.shape
    t = SB_TILE
    r = jnp.arange(2 * t)
    m2 = (((r[:, None] % t) > r[None, :]) | (r[None, :] >= t)).astype(BF16)
    spec = pl.BlockSpec((1, s, LANES), lambda i, j: (i, 0, j))
    return pl.pallas_call(
        _sb_attn_body,
        grid=(b, width // LANES),
        in_specs=[spec, spec, spec, spec, _full((2 * t, 2 * t))],
        out_specs=spec,
        out_shape=jax.ShapeDtypeStruct((b, s, width), BF16),
        compiler_params=_params("parallel", "parallel"),
        name="sb_attn",
    )(q, k, v, g, m2)


def _out_body(x_ref, a_ref, w_ref, fg_ref, o_ref, *, final):
    y = x_ref[...] + _dot(a_ref[...], w_ref[...])
    o_ref[...] = _rms(y, fg_ref[...]) if final else y


def _out_proj(x2, a2, w_out, final_g, final):
    t, d = x2.shape
    kdim = a2.shape[1]
    tm = 2 * ROW_TILE
    row = lambda n: pl.BlockSpec((tm, n), lambda i: (i, 0))
    return pl.pallas_call(
        functools.partial(_out_body, final=final),
        grid=(t // tm,),
        in_specs=[row(d), row(kdim), _full((kdim, d)), _full((1, d))],
        out_specs=row(d),
        out_shape=jax.ShapeDtypeStruct((t, d), F32),
        compiler_params=_params("parallel"),
        name="out_proj",
    )(x2, a2, w_out.astype(BF16), final_g.reshape(1, d))


def kernel(x, norm_g, final_g, gla_w_in, gla_w_gate2, gla_b_gate, gla_norm_g, gla_w_out,
           mla_w_in, mla_q_norm_g, mla_kv_norm_g, mla_w_uq, mla_w_ukv, mla_w_out,
           sb_w_in, sb_w_out):
    b, s, d = x.shape
    depth = norm_g.shape[0]
    assert s % max(2 * ROW_TILE, MLA_TILE, SB_TILE, GLA_CHUNK) == 0, (b, s, d)
    t = b * s
    x2 = x.reshape(t, d)
    for layer in range(depth):
        kind, j = layer % 3, layer // 3
        final = layer == depth - 1
        if kind == 0:
            qe, kn, kd, v, g, dl = _gla_in(x2, norm_g[layer], gla_w_in[j], gla_w_gate2[j], gla_b_gate[j])
            sh = lambda a: a.reshape(b, s, a.shape[-1])
            y = _gla_mix(sh(qe), sh(kn), sh(kd), sh(v), sh(g), dl.reshape(b, s // GLA_CHUNK, -1), gla_norm_g[j])
            w_out = gla_w_out[j]
        elif kind == 1:
            parts = _mla_in(x2, norm_g[layer], mla_w_in[j], mla_q_norm_g[j], mla_kv_norm_g[j],
                            mla_w_uq[j], mla_w_ukv[j], s)
            y = _mla_attn(*[a.reshape(b, s, a.shape[-1]) for a in parts])
            w_out = mla_w_out[j]
        else:
            parts = _sb_in(x2, norm_g[layer], sb_w_in[j])
            y = _sb_attn(*[a.reshape(b, s, a.shape[-1]) for a in parts])
            w_out = sb_w_out[j]
        x2 = _out_proj(x2, y.reshape(t, -1), w_out, final_g, final)
    return x2.reshape(b, s, d)
```

```python
import functools
import math

import jax
import jax.numpy as jnp
from jax import lax
from jax.experimental import pallas as pl
from jax.experimental.pallas import tpu as pltpu

EPS = 1e-6
F32 = jnp.float32
BF16 = jnp.bfloat16

LANES = 128
VMEM_LIMIT_BYTES = 56 * 1024 * 1024

GLA_HEADS = 4
GLA_CHUNK = 64
GLA_GATE_RANK = 16
GLA_GATE_NORMALIZER = 16.0
GLA_GROUP = 4
MLA_HEADS = 16
MLA_Q_RANK = 256
MLA_KV_RANK = 128
MLA_NOPE = 128
MLA_ROPE = 64
MLA_V = 128
ROPE_THETA = 10000.0
SB_HEADS = 16
SB_HEAD_DIM = 64

ROW_TILE = 256
MLA_TILE = 256
SB_TILE = 128
SB_BAND = 3
SB_GROUP = 4
SB_CUTOFF = 120.0


def _dot(a, b):
    return jnp.dot(a, b, preferred_element_type=F32)


def _dot_nt(a, b):
    return lax.dot_general(a, b, (((1,), (1,)), ((), ())), preferred_element_type=F32)


def _dot_tn(a, b):
    return lax.dot_general(a, b, (((0,), (0,)), ((), ())), preferred_element_type=F32)


def _rms(x, g):
    return x * lax.rsqrt(jnp.mean(x * x, axis=-1, keepdims=True) + EPS) * g


def _log_sigmoid(z):
    return jnp.minimum(z, 0.0) - jnp.log(1.0 + jnp.exp(-jnp.abs(z)))


def _silu(g):
    return g * (1.0 / (1.0 + jnp.exp(-g)))


def _params(*sem):
    return pltpu.CompilerParams(dimension_semantics=sem, vmem_limit_bytes=VMEM_LIMIT_BYTES)


def _full(shape):
    return pl.BlockSpec(shape, lambda *_: (0,) * len(shape))


def _gla_in_body(x_ref, ng_ref, wq_ref, wk_ref, wv_ref, wg_ref, wa_ref, w2_ref, bg_ref, tri_ref,
                 qe_ref, kn_ref, kd_ref, v_ref, g_ref, dl_ref, *, q_scale):
    h = _rms(x_ref[...], ng_ref[...]).astype(BF16)
    q = _dot(h, wq_ref[...]) * q_scale
    k = _dot(h, wk_ref[...])
    v_ref[...] = _dot(h, wv_ref[...]).astype(BF16)
    g_ref[...] = _dot(h, wg_ref[...]).astype(BF16)
    a_low = _dot(h, wa_ref[...]).astype(BF16)
    log_a = _log_sigmoid(_dot(a_low, w2_ref[...]) + bg_ref[...]) * (1.0 / GLA_GATE_NORMALIZER)
    hi = log_a.astype(BF16)
    lo = (log_a - hi.astype(F32)).astype(BF16)
    tri = tri_ref[...]
    b = _dot(tri, hi) + _dot(tri, lo)
    tm, dk = b.shape
    b3 = b.reshape(tm // GLA_CHUNK, GLA_CHUNK, dk)
    b_last = b3[:, GLA_CHUNK - 1:GLA_CHUNK, :]
    qe_ref[...] = (q * jnp.exp(b)).astype(BF16)
    kn_ref[...] = (k * jnp.exp(-b)).astype(BF16)
    k3 = k.reshape(tm // GLA_CHUNK, GLA_CHUNK, dk)
    kd_ref[...] = (k3 * jnp.exp(b_last - b3)).reshape(tm, dk).astype(BF16)
    dl_ref[0] = jnp.exp(b_last).reshape(tm // GLA_CHUNK, dk)


def _gla_in(x2, ng, w_in, w_gate2, b_gate):
    t, d = x2.shape
    key_dim = w_gate2.shape[1]
    val_dim = (w_in.shape[1] - 2 * key_dim - GLA_GATE_RANK) // 2
    dk = key_dim // GLA_HEADS
    tm = ROW_TILE
    wq = w_in[:, :key_dim].astype(BF16)
    wk = w_in[:, key_dim:2 * key_dim].astype(BF16)
    wv = w_in[:, 2 * key_dim:2 * key_dim + val_dim].astype(BF16)
    wg = w_in[:, 2 * key_dim + val_dim:2 * key_dim + 2 * val_dim].astype(BF16)
    wa = jnp.pad(w_in[:, 2 * key_dim + 2 * val_dim:], ((0, 0), (0, LANES - GLA_GATE_RANK))).astype(BF16)
    w2 = jnp.pad(w_gate2, ((0, LANES - GLA_GATE_RANK), (0, 0))).astype(BF16)
    r = jnp.arange(tm)
    tri = ((r[:, None] // GLA_CHUNK == r[None, :] // GLA_CHUNK) & (r[:, None] >= r[None, :])).astype(BF16)
    row = lambda n: pl.BlockSpec((tm, n), lambda i: (i, 0))
    outs = pl.pallas_call(
        functools.partial(_gla_in_body, q_scale=dk ** -0.5),
        grid=(t // tm,),
        in_specs=[row(d), _full((1, d)), _full(wq.shape), _full(wk.shape), _full(wv.shape), _full(wg.shape),
                  _full(wa.shape), _full(w2.shape), _full((1, key_dim)), _full((tm, tm))],
        out_specs=[row(key_dim), row(key_dim), row(key_dim), row(val_dim), row(val_dim),
                   pl.BlockSpec((1, tm // GLA_CHUNK, key_dim), lambda i: (i, 0, 0))],
        out_shape=[jax.ShapeDtypeStruct((t, key_dim), BF16)] * 3 + [jax.ShapeDtypeStruct((t, val_dim), BF16)] * 2
        + [jax.ShapeDtypeStruct((t // tm, tm // GLA_CHUNK, key_dim), F32)],
        compiler_params=_params("parallel"),
        name="gla_in",
    )(x2, ng.reshape(1, d), wq, wk, wv, wg, wa, w2, b_gate.reshape(1, key_dim), tri)
    return outs


def _gla_mix_body(qe_ref, kn_ref, kd_ref, v_ref, g_ref, dl_ref, gn_ref, o_ref, oi_ref, u_ref):
    n_chunks = dl_ref.shape[1]
    c = GLA_CHUNK
    grp = GLA_GROUP * c
    ri = lax.broadcasted_iota(jnp.int32, (grp, grp), 0)
    ci = lax.broadcasted_iota(jnp.int32, (grp, grp), 1)
    causal = (ri // c == ci // c) & (ri >= ci)
    dv, dk = u_ref.shape[2], u_ref.shape[1]

    for gi in range(n_chunks // GLA_GROUP):
        r = slice(gi * grp, (gi + 1) * grp)
        v = v_ref[0, r, :]
        attn = jnp.where(causal, _dot_nt(qe_ref[0, r, :], kn_ref[0, r, :]), 0.0).astype(BF16)
        oi_ref[r, :] = _dot(attn, v)
        for i in range(gi * GLA_GROUP, (gi + 1) * GLA_GROUP):
            rc = slice(i * c, (i + 1) * c)
            u_ref[i] = _dot_tn(kd_ref[0, rc, :], v_ref[0, rc, :])

    gn = gn_ref[...]
    decay_t = dl_ref[0].T
    state = jnp.zeros((dk, dv), F32)
    for i in range(n_chunks):
        rc = slice(i * c, (i + 1) * c)
        o = oi_ref[rc, :] + _dot(qe_ref[0, rc, :], state.astype(BF16))
        y = _rms(o, gn) * _silu(g_ref[0, rc, :].astype(F32))
        o_ref[0, rc, :] = y.astype(BF16)
        state = decay_t[:, i:i + 1] * state + u_ref[i]


def _gla_mix(qe, kn, kd, v, g, dlast, gn):
    b, s, key_dim = qe.shape
    val_dim = v.shape[2]
    dk, dv = key_dim // GLA_HEADS, val_dim // GLA_HEADS
    n_chunks = s // GLA_CHUNK
    kspec = pl.BlockSpec((1, s, dk), lambda i, j: (i, 0, j))
    vspec = pl.BlockSpec((1, s, dv), lambda i, j: (i, 0, j))
    return pl.pallas_call(
        _gla_mix_body,
        grid=(b, GLA_HEADS),
        in_specs=[kspec, kspec, kspec, vspec, vspec,
                  pl.BlockSpec((1, n_chunks, dk), lambda i, j: (i, 0, j)), _full((1, dv))],
        out_specs=vspec,
        out_shape=jax.ShapeDtypeStruct((b, s, val_dim), BF16),
        scratch_shapes=[pltpu.VMEM((s, dv), F32), pltpu.VMEM((n_chunks, dk, dv), F32)],
        compiler_params=_params("parallel", "parallel"),
        name="gla_mix",
    )(qe, kn, kd, v, g, dlast, gn.reshape(1, dv))


def _rope(x, cos, sin_signed):
    return x * cos + pltpu.roll(x, 64, 1) * sin_signed


def _mla_in_body(x_ref, ng_ref, wa_ref, wg_ref, gq_ref, gkv_ref, wqn_ref, wqr_ref, wkn_ref, wv_ref,
                 cos_ref, sin_ref, qn_ref, qr_ref, kn_ref, kr_ref, v_ref, g_ref, *, scale):
    h = _rms(x_ref[...], ng_ref[...]).astype(BF16)
    pa = _dot(h, wa_ref[...])
    g_ref[...] = _dot(h, wg_ref[...]).astype(BF16)
    cos, sin = cos_ref[...], sin_ref[...]
    c_q = _rms(pa[:, :MLA_Q_RANK], gq_ref[...]).astype(BF16)
    c_kv = _rms(pa[:, MLA_Q_RANK:MLA_Q_RANK + MLA_KV_RANK], gkv_ref[...]).astype(BF16)
    kr_ref[...] = _rope(pa[:, MLA_Q_RANK + MLA_KV_RANK:], cos, sin).astype(BF16)
    qn_ref[...] = (_dot(c_q, wqn_ref[...]) * scale).astype(BF16)
    qr = _dot(c_q, wqr_ref[...])
    for p in range(qr.shape[1] // LANES):
        sl = slice(p * LANES, (p + 1) * LANES)
        qr_ref[:, sl] = (_rope(qr[:, sl], cos, sin) * scale).astype(BF16)
    kn_ref[...] = _dot(c_kv, wkn_ref[...]).astype(BF16)
    v_ref[...] = _dot(c_kv, wv_ref[...]).astype(BF16)


def _mla_in(x2, ng, w_in, gq, gkv, w_uq, w_ukv, seq):
    t, d = x2.shape
    tm = ROW_TILE
    hq, half = MLA_HEADS, MLA_ROPE // 2
    o_kv, o_kr, o_g = MLA_Q_RANK, MLA_Q_RANK + MLA_KV_RANK, MLA_Q_RANK + MLA_KV_RANK + MLA_ROPE
    kr1, kr2 = w_in[:, o_kr:o_kr + half], w_in[:, o_kr + half:o_g]
    wa = jnp.concatenate([w_in[:, :o_kr], kr1, kr1, kr2, kr2], axis=1).astype(BF16)
    wg = w_in[:, o_g:].astype(BF16)
    uq = w_uq.reshape(MLA_Q_RANK, hq, MLA_NOPE + MLA_ROPE)
    wqn = uq[:, :, :MLA_NOPE].reshape(MLA_Q_RANK, hq * MLA_NOPE).astype(BF16)
    x1 = uq[:, :, MLA_NOPE:MLA_NOPE + half].reshape(MLA_Q_RANK, hq // 2, 2 * half)
    x2r = uq[:, :, MLA_NOPE + half:].reshape(MLA_Q_RANK, hq // 2, 2 * half)
    wqr = jnp.concatenate([x1, x2r], axis=2).reshape(MLA_Q_RANK, hq * MLA_ROPE).astype(BF16)
    ukv = w_ukv.reshape(MLA_KV_RANK, hq, MLA_NOPE + MLA_V)
    wkn = ukv[:, :, :MLA_NOPE].reshape(MLA_KV_RANK, hq * MLA_NOPE).astype(BF16)
    wv = ukv[:, :, MLA_NOPE:].reshape(MLA_KV_RANK, hq * MLA_V).astype(BF16)

    pos = jnp.arange(seq, dtype=F32)
    inv = ROPE_THETA ** (-jnp.arange(0, MLA_ROPE, 2, dtype=F32) / MLA_ROPE)
    ang = pos[:, None] * inv[None, :]
    cos, sin = jnp.cos(ang), jnp.sin(ang)
    cos4 = jnp.concatenate([cos, cos, cos, cos], axis=1)
    sin4 = jnp.concatenate([-sin, -sin, sin, sin], axis=1)

    scale = (MLA_NOPE + MLA_ROPE) ** -0.5
    row = lambda n: pl.BlockSpec((tm, n), lambda i: (i, 0))
    tab = pl.BlockSpec((tm, LANES), lambda i: (i % (seq // tm), 0))
    wide = hq * MLA_NOPE
    return pl.pallas_call(
        functools.partial(_mla_in_body, scale=scale),
        grid=(t // tm,),
        in_specs=[row(d), _full((1, d)), _full(wa.shape), _full(wg.shape), _full((1, MLA_Q_RANK)),
                  _full((1, MLA_KV_RANK)), _full(wqn.shape), _full(wqr.shape), _full(wkn.shape),
                  _full(wv.shape), tab, tab],
        out_specs=[row(wide), row(hq * MLA_ROPE), row(wide), row(LANES), row(wide), row(wide)],
        out_shape=[jax.ShapeDtypeStruct((t, wide), BF16), jax.ShapeDtypeStruct((t, hq * MLA_ROPE), BF16),
                   jax.ShapeDtypeStruct((t, wide), BF16), jax.ShapeDtypeStruct((t, LANES), BF16),
                   jax.ShapeDtypeStruct((t, wide), BF16), jax.ShapeDtypeStruct((t, wide), BF16)],
        compiler_params=_params("parallel"),
        name="mla_in",
    )(x2, ng.reshape(1, d), wa, wg, gq.reshape(1, -1), gkv.reshape(1, -1), wqn, wqr, wkn, wv, cos4, sin4)


def _mla_attn_body(qn_ref, qr_ref, kn_ref, kr_ref, v_ref, g_ref, o_ref, qc_ref, kc_ref, v1_ref):
    s = qn_ref.shape[1]
    t = MLA_TILE
    n_tiles = s // t
    lane = lax.broadcasted_iota(jnp.int32, (s, LANES), 1)
    mine = ((lane // (MLA_ROPE // 2)) % 2) == (pl.program_id(1) % 2)
    qc_ref[:, :LANES] = qn_ref[0]
    qc_ref[:, LANES:] = jnp.where(mine, qr_ref[0], jnp.zeros((), BF16))
    kc_ref[:, :LANES] = kn_ref[0]
    kc_ref[:, LANES:] = kr_ref[0]
    v1_ref[:, :LANES] = v_ref[0]
    v1_ref[:, LANES:] = jnp.ones((s, LANES), BF16)
    ri = lax.broadcasted_iota(jnp.int32, (t, t), 0)
    ci = lax.broadcasted_iota(jnp.int32, (t, t), 1)
    causal = ri >= ci

    for i in range(n_tiles):
        rq = slice(i * t, (i + 1) * t)
        q = qc_ref[rq, :]
        s_diag = jnp.where(causal, _dot_nt(q, kc_ref[rq, :]), -jnp.inf)
        m = jnp.max(s_diag, axis=-1, keepdims=True)
        if i > 0:
            s_past = _dot_nt(q, kc_ref[:i * t, :])
            m = jnp.maximum(m, jnp.max(s_past, axis=-1, keepdims=True))
            acc = _dot(jnp.exp(s_past - m).astype(BF16), v1_ref[:i * t, :])
        else:
            acc = jnp.zeros((t, 2 * LANES), F32)
        acc = acc + _dot(jnp.exp(s_diag - m).astype(BF16), v1_ref[rq, :])
        o = acc[:, :LANES] / acc[:, LANES:]
        o_ref[0, rq, :] = (o * _silu(g_ref[0, rq, :].astype(F32))).astype(BF16)


def _mla_attn(qn, qr, kn, kr, v, g):
    b, s, wide = qn.shape
    spec = pl.BlockSpec((1, s, LANES), lambda i, j: (i, 0, j))
    return pl.pallas_call(
        _mla_attn_body,
        grid=(b, MLA_HEADS),
        in_specs=[spec, pl.BlockSpec((1, s, LANES), lambda i, j: (i, 0, j // 2)), spec,
                  pl.BlockSpec((1, s, LANES), lambda i, j: (i, 0, 0)), spec, spec],
        out_specs=spec,
        out_shape=jax.ShapeDtypeStruct((b, s, wide), BF16),
        scratch_shapes=[pltpu.VMEM((s, 2 * LANES), BF16)] * 3,
        compiler_params=_params("parallel", "parallel"),
        name="mla_attn",
    )(qn, qr, kn, kr, v, g)


def _sb_in_body(x_ref, ng_ref, wq_ref, wk_ref, wv_ref, wg_ref, q_ref, k_ref, v_ref, g_ref, *, scale):
    h = _rms(x_ref[...], ng_ref[...]).astype(BF16)
    q_ref[...] = (_dot(h, wq_ref[...]) * scale).astype(BF16)
    k_ref[...] = _dot(h, wk_ref[...]).astype(BF16)
    v_ref[...] = _dot(h, wv_ref[...]).astype(BF16)
    g_ref[...] = _dot(h, wg_ref[...]).astype(BF16)


def _sb_in(x2, ng, w_in):
    t, d = x2.shape
    width = w_in.shape[1] // 4
    tm = ROW_TILE
    ws = [w_in[:, i * width:(i + 1) * width].astype(BF16) for i in range(4)]
    row = lambda n: pl.BlockSpec((tm, n), lambda i: (i, 0))
    return pl.pallas_call(
        functools.partial(_sb_in_body, scale=SB_HEAD_DIM ** -0.5),
        grid=(t // tm,),
        in_specs=[row(d), _full((1, d))] + [_full(w.shape) for w in ws],
        out_specs=[row(width)] * 4,
        out_shape=[jax.ShapeDtypeStruct((t, width), BF16)] * 4,
        compiler_params=_params("parallel"),
        name="sb_in",
    )(x2, ng.reshape(1, d), *ws)


def _sb_attn_body(q_ref, k_ref, v_ref, g_ref, m2_ref, o_ref, kp_ref, vp_ref):
    s = q_ref.shape[1]
    t = SB_TILE
    n_tiles = s // t
    ri = lax.broadcasted_iota(jnp.int32, (t, t), 0)
    ci = lax.broadcasted_iota(jnp.int32, (t, t), 1)
    strict = ri > ci
    lane = lax.broadcasted_iota(jnp.int32, (t, LANES), 1)
    first = lane < SB_HEAD_DIM
    m2 = m2_ref[...]

    def stats(z, masked):
        log_beta = _log_sigmoid(z)
        log_1m = log_beta - z
        if masked:
            log_1m = jnp.where(strict, log_1m, 0.0)
        hi = log_1m.astype(BF16)
        lo = (log_1m - hi.astype(F32)).astype(BF16)
        res = _dot(jnp.concatenate([hi, lo], axis=1), m2)
        return log_beta, res[:, :t], res[:, t:]

    def weights(log_beta, after, carry, masked):
        a = jnp.exp(log_beta + after + carry)
        if masked:
            a = jnp.where(strict, a, 0.0)
        return a.astype(BF16)

    pad = (SB_BAND - 1) * t
    kp_ref[:pad, :] = jnp.zeros((pad, LANES), BF16)
    vp_ref[:pad, :] = jnp.zeros((pad, LANES), BF16)
    kp_ref[pad:, :] = k_ref[0]
    vp_ref[pad:, :] = v_ref[0]

    def band(q2, i):
        rk = pl.ds(pl.multiple_of(i * t, t), SB_BAND * t)
        k, v = kp_ref[rk, :], vp_ref[rk, :]
        z2 = _dot_nt(q2, k)
        out = []
        for hh in range(2):
            zs = z2[hh * t:(hh + 1) * t]
            parts = [stats(zs[:, u * t:(u + 1) * t], u == SB_BAND - 1) for u in range(SB_BAND)]
            carry = jnp.zeros((t, t), F32)
            a_parts = [None] * SB_BAND
            for u in reversed(range(SB_BAND)):
                log_beta, after, row_sum = parts[u]
                a_parts[u] = weights(log_beta, after, carry, u == SB_BAND - 1)
                carry = carry + row_sum
            out += [carry, _dot(jnp.concatenate(a_parts, axis=1), v)]
        return tuple(out)

    def tail(q2, j0, st):
        def cond(c):
            return (c[0] >= 0) & (jnp.max(jnp.maximum(c[1], c[3])) > -SB_CUTOFF)

        def body(c):
            j = c[0]
            rk = pl.ds(pl.multiple_of(j * t + pad, t), t)
            k, v = kp_ref[rk, :], vp_ref[rk, :]
            z2 = _dot_nt(q2, k)
            new = []
            for hh in range(2):
                carry, acc = c[1 + 2 * hh], c[2 + 2 * hh]
                log_beta, after, row_sum = stats(z2[hh * t:(hh + 1) * t], False)
                new += [carry + row_sum, acc + _dot(weights(log_beta, after, carry, False), v)]
            return (j - 1, *new)

        return lax.while_loop(cond, body, (j0, *st))[1:]

    def masked_queries(rq):
        q = q_ref[0, rq, :]
        zero = jnp.zeros((), BF16)
        return jnp.concatenate([jnp.where(first, q, zero), jnp.where(first, zero, q)], axis=0)

    def finish(rq, st):
        o = jnp.where(first, st[1], st[3])
        o_ref[0, rq, :] = (o * _silu(g_ref[0, rq, :].astype(F32))).astype(BF16)

    def group(gi, c0):
        tiles = []
        worst = None
        for u in range(SB_GROUP):
            i = gi * SB_GROUP + u
            rq = pl.ds(pl.multiple_of(i * t, t), t)
            q2 = masked_queries(rq)
            st = band(q2, i)
            finish(rq, st)
            tiles.append((i, rq, q2, st))
            c = jnp.maximum(st[0], st[2])
            worst = c if worst is None else jnp.maximum(worst, c)

        @pl.when(jnp.max(worst) > -SB_CUTOFF)
        def _():
            for i, rq, q2, st in tiles:
                finish(rq, tail(q2, i - SB_BAND, st))

        return c0

    lax.fori_loop(0, n_tiles // SB_GROUP, group, 0)


def _sb_attn(q, k, v, g):
    b, s, width = q.shape
    t = SB_TILE
    assert s % (SB_GROUP * t) == 0, s
    r = jnp.arange(2 * t)
    m2 = (((r[:, None] % t) > r[None, :]) | (r[None, :] >= t)).astype(BF16)
    spec = pl.BlockSpec((1, s, LANES), lambda i, j: (i, 0, j))
    return pl.pallas_call(
        _sb_attn_body,
        grid=(b, width // LANES),
        in_specs=[spec, spec, spec, spec, _full((2 * t, 2 * t))],
        out_specs=spec,
        out_shape=jax.ShapeDtypeStruct((b, s, width), BF16),
        scratch_shapes=[pltpu.VMEM((s + (SB_BAND - 1) * t, LANES), BF16)] * 2,
        compiler_params=_params("parallel", "parallel"),
        name="sb_attn",
    )(q, k, v, g, m2)


def _out_body(x_ref, a_ref, w_ref, fg_ref, o_ref, *, final):
    y = x_ref[...] + _dot(a_ref[...], w_ref[...])
    o_ref[...] = _rms(y, fg_ref[...]) if final else y


def _out_proj(x2, a2, w_out, final_g, final):
    t, d = x2.shape
    kdim = a2.shape[1]
    tm = 2 * ROW_TILE
    row = lambda n: pl.BlockSpec((tm, n), lambda i: (i, 0))
    return pl.pallas_call(
        functools.partial(_out_body, final=final),
        grid=(t // tm,),
        in_specs=[row(d), row(kdim), _full((kdim, d)), _full((1, d))],
        out_specs=row(d),
        out_shape=jax.ShapeDtypeStruct((t, d), F32),
        compiler_params=_params("parallel"),
        name="out_proj",
    )(x2, a2, w_out.astype(BF16), final_g.reshape(1, d))


def kernel(x, norm_g, final_g, gla_w_in, gla_w_gate2, gla_b_gate, gla_norm_g, gla_w_out,
           mla_w_in, mla_q_norm_g, mla_kv_norm_g, mla_w_uq, mla_w_ukv, mla_w_out,
           sb_w_in, sb_w_out):
    b, s, d = x.shape
    depth = norm_g.shape[0]
    assert s % max(2 * ROW_TILE, MLA_TILE, SB_TILE, GLA_CHUNK) == 0, (b, s, d)
    t = b * s
    x2 = x.reshape(t, d)
    for layer in range(depth):
        kind, j = layer % 3, layer // 3
        final = layer == depth - 1
        if kind == 0:
            qe, kn, kd, v, g, dl = _gla_in(x2, norm_g[layer], gla_w_in[j], gla_w_gate2[j], gla_b_gate[j])
            sh = lambda a: a.reshape(b, s, a.shape[-1])
            y = _gla_mix(sh(qe), sh(kn), sh(kd), sh(v), sh(g), dl.reshape(b, s // GLA_CHUNK, -1), gla_norm_g[j])
            w_out = gla_w_out[j]
        elif kind == 1:
            parts = _mla_in(x2, norm_g[layer], mla_w_in[j], mla_q_norm_g[j], mla_kv_norm_g[j],
                            mla_w_uq[j], mla_w_ukv[j], s)
            y = _mla_attn(*[a.reshape(b, s, a.shape[-1]) for a in parts])
            w_out = mla_w_out[j]
        else:
            parts = _sb_in(x2, norm_g[layer], sb_w_in[j])
            y = _sb_attn(*[a.reshape(b, s, a.shape[-1]) for a in parts])
            w_out = sb_w_out[j]
        x2 = _out_proj(x2, y.reshape(t, -1), w_out, final_g, final)
    return x2.reshape(b, s, d)
```

```python
import functools
import math

import jax
import jax.numpy as jnp
from jax import lax
from jax.experimental import pallas as pl
from jax.experimental.pallas import tpu as pltpu

EPS = 1e-6
F32 = jnp.float32
BF16 = jnp.bfloat16

LANES = 128
VMEM_LIMIT_BYTES = 56 * 1024 * 1024

GLA_HEADS = 4
GLA_CHUNK = 64
GLA_GATE_RANK = 16
GLA_GATE_NORMALIZER = 16.0
GLA_GROUP = 4
MLA_HEADS = 16
MLA_Q_RANK = 256
MLA_KV_RANK = 128
MLA_NOPE = 128
MLA_ROPE = 64
MLA_V = 128
ROPE_THETA = 10000.0
SB_HEADS = 16
SB_HEAD_DIM = 64

ROW_TILE = 256
MLA_TILE = 256
SB_TILE = 128
SB_BAND = 3
SB_GROUP = 4
SB_CUTOFF = 120.0


def _dot(a, b):
    return jnp.dot(a, b, preferred_element_type=F32)


def _dot_nt(a, b):
    return lax.dot_general(a, b, (((1,), (1,)), ((), ())), preferred_element_type=F32)


def _dot_tn(a, b):
    return lax.dot_general(a, b, (((0,), (0,)), ((), ())), preferred_element_type=F32)


def _rms(x, g):
    return x * lax.rsqrt(jnp.mean(x * x, axis=-1, keepdims=True) + EPS) * g


def _log_sigmoid(z):
    return jnp.minimum(z, 0.0) - jnp.log(1.0 + jnp.exp(-jnp.abs(z)))


def _silu(g):
    return g * (1.0 / (1.0 + jnp.exp(-g)))


def _params(*sem):
    return pltpu.CompilerParams(dimension_semantics=sem, vmem_limit_bytes=VMEM_LIMIT_BYTES)


def _full(shape):
    return pl.BlockSpec(shape, lambda *_: (0,) * len(shape))


def _row(n):
    return pl.BlockSpec((ROW_TILE, n), lambda i: (i, 0))


def _layer_input(refs, fused):
    if not fused:
        return refs[0][...], refs[1:]
    x_ref, a_ref, wo_ref, *rest, xo_ref = refs
    x = x_ref[...] + _dot(a_ref[...], wo_ref[...])
    xo_ref[...] = x
    return x, rest


def _in_call(body, name, x2, prev, arrays, specs, out_specs, out_shape):
    t, d = x2.shape
    fused = prev is not None
    ins, in_specs = [x2], [_row(d)]
    if fused:
        a2, w_out = prev
        ins += [a2, w_out.astype(BF16)]
        in_specs += [_row(a2.shape[1]), _full(w_out.shape)]
        out_specs = out_specs + [_row(d)]
        out_shape = out_shape + [jax.ShapeDtypeStruct((t, d), F32)]
    outs = pl.pallas_call(
        functools.partial(body, fused=fused),
        grid=(t // ROW_TILE,),
        in_specs=in_specs + specs,
        out_specs=out_specs,
        out_shape=out_shape,
        compiler_params=_params("parallel"),
        name=name,
    )(*ins, *arrays)
    return (outs[:-1], outs[-1]) if fused else (outs, x2)


def _gla_in_body(*refs, q_scale, fused):
    x, refs = _layer_input(refs, fused)
    (ng_ref, wq_ref, wk_ref, wv_ref, wg_ref, wa_ref, w2_ref, bg_ref, tri_ref,
     qe_ref, kn_ref, kd_ref, v_ref, g_ref, dl_ref) = refs
    h = _rms(x, ng_ref[...]).astype(BF16)
    q = _dot(h, wq_ref[...]) * q_scale
    k = _dot(h, wk_ref[...])
    v_ref[...] = _dot(h, wv_ref[...]).astype(BF16)
    g_ref[...] = _dot(h, wg_ref[...]).astype(BF16)
    a_low = _dot(h, wa_ref[...]).astype(BF16)
    log_a = _log_sigmoid(_dot(a_low, w2_ref[...]) + bg_ref[...]) * (1.0 / GLA_GATE_NORMALIZER)
    hi = log_a.astype(BF16)
    lo = (log_a - hi.astype(F32)).astype(BF16)
    tri = tri_ref[...]
    b = _dot(tri, hi) + _dot(tri, lo)
    tm, dk = b.shape
    b3 = b.reshape(tm // GLA_CHUNK, GLA_CHUNK, dk)
    b_last = b3[:, GLA_CHUNK - 1:GLA_CHUNK, :]
    qe_ref[...] = (q * jnp.exp(b)).astype(BF16)
    kn_ref[...] = (k * jnp.exp(-b)).astype(BF16)
    k3 = k.reshape(tm // GLA_CHUNK, GLA_CHUNK, dk)
    kd_ref[...] = (k3 * jnp.exp(b_last - b3)).reshape(tm, dk).astype(BF16)
    dl_ref[0] = jnp.exp(b_last).reshape(tm // GLA_CHUNK, dk)


def _gla_in(x2, prev, ng, w_in, w_gate2, b_gate):
    t, d = x2.shape
    key_dim = w_gate2.shape[1]
    val_dim = (w_in.shape[1] - 2 * key_dim - GLA_GATE_RANK) // 2
    dk = key_dim // GLA_HEADS
    tm = ROW_TILE
    wq = w_in[:, :key_dim].astype(BF16)
    wk = w_in[:, key_dim:2 * key_dim].astype(BF16)
    wv = w_in[:, 2 * key_dim:2 * key_dim + val_dim].astype(BF16)
    wg = w_in[:, 2 * key_dim + val_dim:2 * key_dim + 2 * val_dim].astype(BF16)
    wa = jnp.pad(w_in[:, 2 * key_dim + 2 * val_dim:], ((0, 0), (0, LANES - GLA_GATE_RANK))).astype(BF16)
    w2 = jnp.pad(w_gate2, ((0, LANES - GLA_GATE_RANK), (0, 0))).astype(BF16)
    r = jnp.arange(tm)
    tri = ((r[:, None] // GLA_CHUNK == r[None, :] // GLA_CHUNK) & (r[:, None] >= r[None, :])).astype(BF16)
    arrays = [ng.reshape(1, d), wq, wk, wv, wg, wa, w2, b_gate.reshape(1, key_dim), tri]
    return _in_call(
        functools.partial(_gla_in_body, q_scale=dk ** -0.5), "gla_in", x2, prev, arrays,
        [_full(a.shape) for a in arrays],
        [_row(key_dim), _row(key_dim), _row(key_dim), _row(val_dim), _row(val_dim),
         pl.BlockSpec((1, tm // GLA_CHUNK, key_dim), lambda i: (i, 0, 0))],
        [jax.ShapeDtypeStruct((t, key_dim), BF16)] * 3 + [jax.ShapeDtypeStruct((t, val_dim), BF16)] * 2
        + [jax.ShapeDtypeStruct((t // tm, tm // GLA_CHUNK, key_dim), F32)])


def _gla_mix_body(qe_ref, kn_ref, kd_ref, v_ref, g_ref, dl_ref, gn_ref, o_ref, oi_ref, u_ref):
    n_chunks = dl_ref.shape[1]
    c = GLA_CHUNK
    grp = GLA_GROUP * c
    ri = lax.broadcasted_iota(jnp.int32, (grp, grp), 0)
    ci = lax.broadcasted_iota(jnp.int32, (grp, grp), 1)
    causal = (ri // c == ci // c) & (ri >= ci)
    dv, dk = u_ref.shape[2], u_ref.shape[1]

    for gi in range(n_chunks // GLA_GROUP):
        r = slice(gi * grp, (gi + 1) * grp)
        v = v_ref[0, r, :]
        attn = jnp.where(causal, _dot_nt(qe_ref[0, r, :], kn_ref[0, r, :]), 0.0).astype(BF16)
        oi_ref[r, :] = _dot(attn, v)
        for i in range(gi * GLA_GROUP, (gi + 1) * GLA_GROUP):
            rc = slice(i * c, (i + 1) * c)
            u_ref[i] = _dot_tn(kd_ref[0, rc, :], v_ref[0, rc, :])

    gn = gn_ref[...]
    decay_t = dl_ref[0].T
    state = jnp.zeros((dk, dv), F32)
    for i in range(n_chunks):
        rc = slice(i * c, (i + 1) * c)
        o = oi_ref[rc, :] + _dot(qe_ref[0, rc, :], state.astype(BF16))
        y = _rms(o, gn) * _silu(g_ref[0, rc, :].astype(F32))
        o_ref[0, rc, :] = y.astype(BF16)
        state = decay_t[:, i:i + 1] * state + u_ref[i]


def _gla_mix(qe, kn, kd, v, g, dlast, gn):
    b, s, key_dim = qe.shape
    val_dim = v.shape[2]
    dk, dv = key_dim // GLA_HEADS, val_dim // GLA_HEADS
    n_chunks = s // GLA_CHUNK
    kspec = pl.BlockSpec((1, s, dk), lambda i, j: (i, 0, j))
    vspec = pl.BlockSpec((1, s, dv), lambda i, j: (i, 0, j))
    return pl.pallas_call(
        _gla_mix_body,
        grid=(b, GLA_HEADS),
        in_specs=[kspec, kspec, kspec, vspec, vspec,
                  pl.BlockSpec((1, n_chunks, dk), lambda i, j: (i, 0, j)), _full((1, dv))],
        out_specs=vspec,
        out_shape=jax.ShapeDtypeStruct((b, s, val_dim), BF16),
        scratch_shapes=[pltpu.VMEM((s, dv), F32), pltpu.VMEM((n_chunks, dk, dv), F32)],
        compiler_params=_params("parallel", "parallel"),
        name="gla_mix",
    )(qe, kn, kd, v, g, dlast, gn.reshape(1, dv))


def _rope(x, cos, sin_signed):
    return x * cos + pltpu.roll(x, 64, 1) * sin_signed


def _mla_in_body(*refs, scale, fused):
    x, refs = _layer_input(refs, fused)
    (ng_ref, wa_ref, wg_ref, gq_ref, gkv_ref, wqn_ref, wqr_ref, wkn_ref, wv_ref,
     cos_ref, sin_ref, qn_ref, qr_ref, kn_ref, kr_ref, v_ref, g_ref) = refs
    h = _rms(x, ng_ref[...]).astype(BF16)
    pa = _dot(h, wa_ref[...])
    g_ref[...] = _dot(h, wg_ref[...]).astype(BF16)
    cos, sin = cos_ref[...], sin_ref[...]
    c_q = _rms(pa[:, :MLA_Q_RANK], gq_ref[...]).astype(BF16)
    c_kv = _rms(pa[:, MLA_Q_RANK:MLA_Q_RANK + MLA_KV_RANK], gkv_ref[...]).astype(BF16)
    kr_ref[...] = _rope(pa[:, MLA_Q_RANK + MLA_KV_RANK:], cos, sin).astype(BF16)
    qn_ref[...] = (_dot(c_q, wqn_ref[...]) * scale).astype(BF16)
    qr = _dot(c_q, wqr_ref[...])
    for p in range(qr.shape[1] // LANES):
        sl = slice(p * LANES, (p + 1) * LANES)
        qr_ref[:, sl] = (_rope(qr[:, sl], cos, sin) * scale).astype(BF16)
    kn_ref[...] = _dot(c_kv, wkn_ref[...]).astype(BF16)
    v_ref[...] = _dot(c_kv, wv_ref[...]).astype(BF16)


def _mla_in(x2, prev, ng, w_in, gq, gkv, w_uq, w_ukv, seq):
    t, d = x2.shape
    tm = ROW_TILE
    hq, half = MLA_HEADS, MLA_ROPE // 2
    o_kv, o_kr, o_g = MLA_Q_RANK, MLA_Q_RANK + MLA_KV_RANK, MLA_Q_RANK + MLA_KV_RANK + MLA_ROPE
    kr1, kr2 = w_in[:, o_kr:o_kr + half], w_in[:, o_kr + half:o_g]
    wa = jnp.concatenate([w_in[:, :o_kr], kr1, kr1, kr2, kr2], axis=1).astype(BF16)
    wg = w_in[:, o_g:].astype(BF16)
    uq = w_uq.reshape(MLA_Q_RANK, hq, MLA_NOPE + MLA_ROPE)
    wqn = uq[:, :, :MLA_NOPE].reshape(MLA_Q_RANK, hq * MLA_NOPE).astype(BF16)
    x1 = uq[:, :, MLA_NOPE:MLA_NOPE + half].reshape(MLA_Q_RANK, hq // 2, 2 * half)
    x2r = uq[:, :, MLA_NOPE + half:].reshape(MLA_Q_RANK, hq // 2, 2 * half)
    wqr = jnp.concatenate([x1, x2r], axis=2).reshape(MLA_Q_RANK, hq * MLA_ROPE).astype(BF16)
    ukv = w_ukv.reshape(MLA_KV_RANK, hq, MLA_NOPE + MLA_V)
    wkn = ukv[:, :, :MLA_NOPE].reshape(MLA_KV_RANK, hq * MLA_NOPE).astype(BF16)
    wv = ukv[:, :, MLA_NOPE:].reshape(MLA_KV_RANK, hq * MLA_V).astype(BF16)

    pos = jnp.arange(seq, dtype=F32)
    inv = ROPE_THETA ** (-jnp.arange(0, MLA_ROPE, 2, dtype=F32) / MLA_ROPE)
    ang = pos[:, None] * inv[None, :]
    cos, sin = jnp.cos(ang), jnp.sin(ang)
    cos4 = jnp.concatenate([cos, cos, cos, cos], axis=1)
    sin4 = jnp.concatenate([-sin, -sin, sin, sin], axis=1)

    scale = (MLA_NOPE + MLA_ROPE) ** -0.5
    tab = pl.BlockSpec((tm, LANES), lambda i: (i % (seq // tm), 0))
    wide = hq * MLA_NOPE
    arrays = [ng.reshape(1, d), wa, wg, gq.reshape(1, -1), gkv.reshape(1, -1), wqn, wqr, wkn, wv]
    widths = [wide, hq * MLA_ROPE, wide, LANES, wide, wide]
    return _in_call(
        functools.partial(_mla_in_body, scale=scale), "mla_in", x2, prev, arrays + [cos4, sin4],
        [_full(a.shape) for a in arrays] + [tab, tab],
        [_row(n) for n in widths], [jax.ShapeDtypeStruct((t, n), BF16) for n in widths])


def _mla_attn_body(qn_ref, qr_ref, kn_ref, kr_ref, v_ref, g_ref, o_ref, qc_ref, kc_ref, v1_ref):
    s = qn_ref.shape[1]
    t = MLA_TILE
    n_tiles = s // t
    lane = lax.broadcasted_iota(jnp.int32, (s, LANES), 1)
    for hh in range(2):
        hl = slice(hh * LANES, (hh + 1) * LANES)
        mine = ((lane // (MLA_ROPE // 2)) % 2) == hh
        qc_ref[hh, :, :LANES] = qn_ref[0, :, hl]
        qc_ref[hh, :, LANES:] = jnp.where(mine, qr_ref[0], jnp.zeros((), BF16))
        kc_ref[hh, :, :LANES] = kn_ref[0, :, hl]
        kc_ref[hh, :, LANES:] = kr_ref[0]
        v1_ref[hh, :, :LANES] = v_ref[0, :, hl]
        v1_ref[hh, :, LANES:] = jnp.ones((s, LANES), BF16)
    ri = lax.broadcasted_iota(jnp.int32, (t, t), 0)
    ci = lax.broadcasted_iota(jnp.int32, (t, t), 1)
    causal = ri >= ci

    for i in range(n_tiles):
        rq = slice(i * t, (i + 1) * t)
        for hh in range(2):
            hl = slice(hh * LANES, (hh + 1) * LANES)
            q = qc_ref[hh, rq, :]
            s_diag = jnp.where(causal, _dot_nt(q, kc_ref[hh, rq, :]), -jnp.inf)
            m = jnp.max(s_diag, axis=-1, keepdims=True)
            if i > 0:
                s_past = _dot_nt(q, kc_ref[hh, :i * t, :])
                m = jnp.maximum(m, jnp.max(s_past, axis=-1, keepdims=True))
                acc = _dot(jnp.exp(s_past - m).astype(BF16), v1_ref[hh, :i * t, :])
            else:
                acc = jnp.zeros((t, 2 * LANES), F32)
            acc = acc + _dot(jnp.exp(s_diag - m).astype(BF16), v1_ref[hh, rq, :])
            o = acc[:, :LANES] / acc[:, LANES:]
            o_ref[0, rq, hl] = (o * _silu(g_ref[0, rq, hl].astype(F32))).astype(BF16)


def _mla_attn(qn, qr, kn, kr, v, g):
    b, s, wide = qn.shape
    pair = pl.BlockSpec((1, s, 2 * LANES), lambda i, j: (i, 0, j))
    return pl.pallas_call(
        _mla_attn_body,
        grid=(b, MLA_HEADS // 2),
        in_specs=[pair, pl.BlockSpec((1, s, LANES), lambda i, j: (i, 0, j)), pair,
                  pl.BlockSpec((1, s, LANES), lambda i, j: (i, 0, 0)), pair, pair],
        out_specs=pair,
        out_shape=jax.ShapeDtypeStruct((b, s, wide), BF16),
        scratch_shapes=[pltpu.VMEM((2, s, 2 * LANES), BF16)] * 3,
        compiler_params=_params("parallel", "parallel"),
        name="mla_attn",
    )(qn, qr, kn, kr, v, g)


def _sb_in_body(*refs, scale, fused):
    x, refs = _layer_input(refs, fused)
    ng_ref, wq_ref, wk_ref, wv_ref, wg_ref, q_ref, k_ref, v_ref, g_ref = refs
    h = _rms(x, ng_ref[...]).astype(BF16)
    q_ref[...] = (_dot(h, wq_ref[...]) * scale).astype(BF16)
    k_ref[...] = _dot(h, wk_ref[...]).astype(BF16)
    v_ref[...] = _dot(h, wv_ref[...]).astype(BF16)
    g_ref[...] = _dot(h, wg_ref[...]).astype(BF16)


def _sb_in(x2, prev, ng, w_in):
    t, d = x2.shape
    width = w_in.shape[1] // 4
    arrays = [ng.reshape(1, d)] + [w_in[:, i * width:(i + 1) * width].astype(BF16) for i in range(4)]
    return _in_call(
        functools.partial(_sb_in_body, scale=SB_HEAD_DIM ** -0.5), "sb_in", x2, prev, arrays,
        [_full(a.shape) for a in arrays], [_row(width)] * 4, [jax.ShapeDtypeStruct((t, width), BF16)] * 4)


def _sb_attn_body(q_ref, k_ref, v_ref, g_ref, m2_ref, o_ref, kp_ref, vp_ref):
    s = q_ref.shape[1]
    t = SB_TILE
    n_tiles = s // t
    ri = lax.broadcasted_iota(jnp.int32, (t, t), 0)
    ci = lax.broadcasted_iota(jnp.int32, (t, t), 1)
    strict = ri > ci
    lane = lax.broadcasted_iota(jnp.int32, (t, LANES), 1)
    first = lane < SB_HEAD_DIM
    m2 = m2_ref[...]

    def stats(z, masked):
        log_beta = _log_sigmoid(z)
        log_1m = log_beta - z
        if masked:
            log_1m = jnp.where(strict, log_1m, 0.0)
        hi = log_1m.astype(BF16)
        lo = (log_1m - hi.astype(F32)).astype(BF16)
        res = _dot(jnp.concatenate([hi, lo], axis=1), m2)
        return log_beta, res[:, :t], res[:, t:]

    def weights(log_beta, after, carry, masked):
        a = jnp.exp(log_beta + after + carry)
        if masked:
            a = jnp.where(strict, a, 0.0)
        return a.astype(BF16)

    pad = (SB_BAND - 1) * t
    kp_ref[:pad, :] = jnp.zeros((pad, LANES), BF16)
    vp_ref[:pad, :] = jnp.zeros((pad, LANES), BF16)
    kp_ref[pad:, :] = k_ref[0]
    vp_ref[pad:, :] = v_ref[0]

    def band(q2, i):
        rk = pl.ds(pl.multiple_of(i * t, t), SB_BAND * t)
        k, v = kp_ref[rk, :], vp_ref[rk, :]
        z2 = _dot_nt(q2, k)
        out = []
        for hh in range(2):
            zs = z2[hh * t:(hh + 1) * t]
            parts = [stats(zs[:, u * t:(u + 1) * t], u == SB_BAND - 1) for u in range(SB_BAND)]
            carry = jnp.zeros((t, t), F32)
            a_parts = [None] * SB_BAND
            for u in reversed(range(SB_BAND)):
                log_beta, after, row_sum = parts[u]
                a_parts[u] = weights(log_beta, after, carry, u == SB_BAND - 1)
                carry = carry + row_sum
            out += [carry, _dot(jnp.concatenate(a_parts, axis=1), v)]
        return tuple(out)

    def tail(q2, j0, st):
        def cond(c):
            return (c[0] >= 0) & (jnp.max(jnp.maximum(c[1], c[3])) > -SB_CUTOFF)

        def body(c):
            j = c[0]
            rk = pl.ds(pl.multiple_of(j * t + pad, t), t)
            k, v = kp_ref[rk, :], vp_ref[rk, :]
            z2 = _dot_nt(q2, k)
            new = []
            for hh in range(2):
                carry, acc = c[1 + 2 * hh], c[2 + 2 * hh]
                log_beta, after, row_sum = stats(z2[hh * t:(hh + 1) * t], False)
                new += [carry + row_sum, acc + _dot(weights(log_beta, after, carry, False), v)]
            return (j - 1, *new)

        return lax.while_loop(cond, body, (j0, *st))[1:]

    def masked_queries(rq):
        q = q_ref[0, rq, :]
        zero = jnp.zeros((), BF16)
        return jnp.concatenate([jnp.where(first, q, zero), jnp.where(first, zero, q)], axis=0)

    def finish(rq, st):
        o = jnp.where(first, st[1], st[3])
        o_ref[0, rq, :] = (o * _silu(g_ref[0, rq, :].astype(F32))).astype(BF16)

    def group(gi, c0):
        tiles = []
        worst = None
        for u in range(SB_GROUP):
            i = gi * SB_GROUP + u
            rq = pl.ds(pl.multiple_of(i * t, t), t)
            q2 = masked_queries(rq)
            st = band(q2, i)
            finish(rq, st)
            tiles.append((i, rq, q2, st))
            c = jnp.maximum(st[0], st[2])
            worst = c if worst is None else jnp.maximum(worst, c)

        @pl.when(jnp.max(worst) > -SB_CUTOFF)
        def _():
            for i, rq, q2, st in tiles:
                finish(rq, tail(q2, i - SB_BAND, st))

        return c0

    lax.fori_loop(0, n_tiles // SB_GROUP, group, 0)


def _sb_attn(q, k, v, g):
    b, s, width = q.shape
    t = SB_TILE
    assert s % (SB_GROUP * t) == 0, s
    r = jnp.arange(2 * t)
    m2 = (((r[:, None] % t) > r[None, :]) | (r[None, :] >= t)).astype(BF16)
    spec = pl.BlockSpec((1, s, LANES), lambda i, j: (i, 0, j))
    return pl.pallas_call(
        _sb_attn_body,
        grid=(b, width // LANES),
        in_specs=[spec, spec, spec, spec, _full((2 * t, 2 * t))],
        out_specs=spec,
        out_shape=jax.ShapeDtypeStruct((b, s, width), BF16),
        scratch_shapes=[pltpu.VMEM((s + (SB_BAND - 1) * t, LANES), BF16)] * 2,
        compiler_params=_params("parallel", "parallel"),
        name="sb_attn",
    )(q, k, v, g, m2)


def _out_body(x_ref, a_ref, w_ref, fg_ref, o_ref):
    o_ref[...] = _rms(x_ref[...] + _dot(a_ref[...], w_ref[...]), fg_ref[...])


def _out_proj(x2, a2, w_out, final_g):
    t, d = x2.shape
    kdim = a2.shape[1]
    tm = 2 * ROW_TILE
    row = lambda n: pl.BlockSpec((tm, n), lambda i: (i, 0))
    return pl.pallas_call(
        _out_body,
        grid=(t // tm,),
        in_specs=[row(d), row(kdim), _full((kdim, d)), _full((1, d))],
        out_specs=row(d),
        out_shape=jax.ShapeDtypeStruct((t, d), F32),
        compiler_params=_params("parallel"),
        name="out_proj",
    )(x2, a2, w_out.astype(BF16), final_g.reshape(1, d))


def kernel(x, norm_g, final_g, gla_w_in, gla_w_gate2, gla_b_gate, gla_norm_g, gla_w_out,
           mla_w_in, mla_q_norm_g, mla_kv_norm_g, mla_w_uq, mla_w_ukv, mla_w_out,
           sb_w_in, sb_w_out):
    b, s, d = x.shape
    depth = norm_g.shape[0]
    assert s % max(2 * ROW_TILE, MLA_TILE, SB_TILE, GLA_CHUNK) == 0, (b, s, d)
    t = b * s
    x2 = x.reshape(t, d)
    sh = lambda a: a.reshape(b, s, a.shape[-1])
    prev = None
    for layer in range(depth):
        kind, j = layer % 3, layer // 3
        if kind == 0:
            (qe, kn, kd, v, g, dl), x2 = _gla_in(x2, prev, norm_g[layer], gla_w_in[j], gla_w_gate2[j],
                                                 gla_b_gate[j])
            y = _gla_mix(sh(qe), sh(kn), sh(kd), sh(v), sh(g), dl.reshape(b, s // GLA_CHUNK, -1), gla_norm_g[j])
            w_out = gla_w_out[j]
        elif kind == 1:
            parts, x2 = _mla_in(x2, prev, norm_g[layer], mla_w_in[j], mla_q_norm_g[j], mla_kv_norm_g[j],
                                mla_w_uq[j], mla_w_ukv[j], s)
            y = _mla_attn(*[sh(a) for a in parts])
            w_out = mla_w_out[j]
        else:
            parts, x2 = _sb_in(x2, prev, norm_g[layer], sb_w_in[j])
            y = _sb_attn(*[sh(a) for a in parts])
            w_out = sb_w_out[j]
        prev = (y.reshape(t, -1), w_out)
    return _out_proj(x2, prev[0], prev[1], final_g).reshape(b, s, d)
```

```python
import functools
import math

import jax
import jax.numpy as jnp
from jax import lax
from jax.experimental import pallas as pl
from jax.experimental.pallas import tpu as pltpu

EPS = 1e-6
F32 = jnp.float32
BF16 = jnp.bfloat16

LANES = 128
VMEM_LIMIT_BYTES = 56 * 1024 * 1024

GLA_HEADS = 4
GLA_CHUNK = 64
GLA_GATE_RANK = 16
GLA_GATE_NORMALIZER = 16.0
GLA_GROUP = 4
MLA_HEADS = 16
MLA_Q_RANK = 256
MLA_KV_RANK = 128
MLA_NOPE = 128
MLA_ROPE = 64
MLA_V = 128
ROPE_THETA = 10000.0
SB_HEADS = 16
SB_HEAD_DIM = 64

ROW_TILE = 256
MLA_TILE = 256
MLA_STEP_HEADS = 4
SB_TILE = 128
SB_BAND = 3
SB_GROUP = 8
SB_CUTOFF = 120.0


def _dot(a, b):
    return jnp.dot(a, b, preferred_element_type=F32)


def _dot_nt(a, b):
    return lax.dot_general(a, b, (((1,), (1,)), ((), ())), preferred_element_type=F32)


def _dot_tn(a, b):
    return lax.dot_general(a, b, (((0,), (0,)), ((), ())), preferred_element_type=F32)


def _rms(x, g):
    return x * lax.rsqrt(jnp.mean(x * x, axis=-1, keepdims=True) + EPS) * g


def _log_sigmoid(z):
    return jnp.minimum(z, 0.0) - jnp.log(1.0 + jnp.exp(-jnp.abs(z)))


def _silu(g):
    return g * (1.0 / (1.0 + jnp.exp(-g)))


def _params(*sem):
    return pltpu.CompilerParams(dimension_semantics=sem, vmem_limit_bytes=VMEM_LIMIT_BYTES)


def _full(shape):
    return pl.BlockSpec(shape, lambda *_: (0,) * len(shape))


def _row(n):
    return pl.BlockSpec((ROW_TILE, n), lambda i: (i, 0))


def _layer_input(refs, fused):
    if not fused:
        return refs[0][...], refs[1:]
    x_ref, a_ref, wo_ref, *rest, xo_ref = refs
    x = x_ref[...] + _dot(a_ref[...], wo_ref[...])
    xo_ref[...] = x
    return x, rest


def _in_call(body, name, x2, prev, arrays, specs, out_specs, out_shape):
    t, d = x2.shape
    fused = prev is not None
    ins, in_specs = [x2], [_row(d)]
    if fused:
        a2, w_out = prev
        ins += [a2, w_out.astype(BF16)]
        in_specs += [_row(a2.shape[1]), _full(w_out.shape)]
        out_specs = out_specs + [_row(d)]
        out_shape = out_shape + [jax.ShapeDtypeStruct((t, d), F32)]
    outs = pl.pallas_call(
        functools.partial(body, fused=fused),
        grid=(t // ROW_TILE,),
        in_specs=in_specs + specs,
        out_specs=out_specs,
        out_shape=out_shape,
        compiler_params=_params("parallel"),
        name=name,
    )(*ins, *arrays)
    return (outs[:-1], outs[-1]) if fused else (outs, x2)


def _gla_in_body(*refs, q_scale, fused):
    x, refs = _layer_input(refs, fused)
    (ng_ref, wq_ref, wk_ref, wv_ref, wg_ref, wa_ref, w2_ref, bg_ref, tri_ref,
     qe_ref, kn_ref, kd_ref, v_ref, g_ref, dl_ref) = refs
    h = _rms(x, ng_ref[...]).astype(BF16)
    q = _dot(h, wq_ref[...]) * q_scale
    k = _dot(h, wk_ref[...])
    v_ref[...] = _dot(h, wv_ref[...]).astype(BF16)
    g_ref[...] = _dot(h, wg_ref[...]).astype(BF16)
    a_low = _dot(h, wa_ref[...]).astype(BF16)
    log_a = _log_sigmoid(_dot(a_low, w2_ref[...]) + bg_ref[...]) * (1.0 / GLA_GATE_NORMALIZER)
    hi = log_a.astype(BF16)
    lo = (log_a - hi.astype(F32)).astype(BF16)
    tri = tri_ref[...]
    b = _dot(tri, hi) + _dot(tri, lo)
    tm, dk = b.shape
    b3 = b.reshape(tm // GLA_CHUNK, GLA_CHUNK, dk)
    b_last = b3[:, GLA_CHUNK - 1:GLA_CHUNK, :]
    qe_ref[...] = (q * jnp.exp(b)).astype(BF16)
    kn_ref[...] = (k * jnp.exp(-b)).astype(BF16)
    k3 = k.reshape(tm // GLA_CHUNK, GLA_CHUNK, dk)
    kd_ref[...] = (k3 * jnp.exp(b_last - b3)).reshape(tm, dk).astype(BF16)
    dl_ref[0] = jnp.exp(b_last).reshape(tm // GLA_CHUNK, dk)


def _gla_in(x2, prev, ng, w_in, w_gate2, b_gate):
    t, d = x2.shape
    key_dim = w_gate2.shape[1]
    val_dim = (w_in.shape[1] - 2 * key_dim - GLA_GATE_RANK) // 2
    dk = key_dim // GLA_HEADS
    tm = ROW_TILE
    wq = w_in[:, :key_dim].astype(BF16)
    wk = w_in[:, key_dim:2 * key_dim].astype(BF16)
    wv = w_in[:, 2 * key_dim:2 * key_dim + val_dim].astype(BF16)
    wg = w_in[:, 2 * key_dim + val_dim:2 * key_dim + 2 * val_dim].astype(BF16)
    wa = jnp.pad(w_in[:, 2 * key_dim + 2 * val_dim:], ((0, 0), (0, LANES - GLA_GATE_RANK))).astype(BF16)
    w2 = jnp.pad(w_gate2, ((0, LANES - GLA_GATE_RANK), (0, 0))).astype(BF16)
    r = jnp.arange(tm)
    tri = ((r[:, None] // GLA_CHUNK == r[None, :] // GLA_CHUNK) & (r[:, None] >= r[None, :])).astype(BF16)
    arrays = [ng.reshape(1, d), wq, wk, wv, wg, wa, w2, b_gate.reshape(1, key_dim), tri]
    return _in_call(
        functools.partial(_gla_in_body, q_scale=dk ** -0.5), "gla_in", x2, prev, arrays,
        [_full(a.shape) for a in arrays],
        [_row(key_dim), _row(key_dim), _row(key_dim), _row(val_dim), _row(val_dim),
         pl.BlockSpec((1, tm // GLA_CHUNK, key_dim), lambda i: (i, 0, 0))],
        [jax.ShapeDtypeStruct((t, key_dim), BF16)] * 3 + [jax.ShapeDtypeStruct((t, val_dim), BF16)] * 2
        + [jax.ShapeDtypeStruct((t // tm, tm // GLA_CHUNK, key_dim), F32)])


def _gla_mix_body(qe_ref, kn_ref, kd_ref, v_ref, g_ref, dl_ref, gn_ref, o_ref, oi_ref, u_ref):
    n_chunks = dl_ref.shape[1]
    c = GLA_CHUNK
    grp = GLA_GROUP * c
    ri = lax.broadcasted_iota(jnp.int32, (grp, grp), 0)
    ci = lax.broadcasted_iota(jnp.int32, (grp, grp), 1)
    causal = (ri // c == ci // c) & (ri >= ci)
    dv, dk = u_ref.shape[2], u_ref.shape[1]

    for gi in range(n_chunks // GLA_GROUP):
        r = slice(gi * grp, (gi + 1) * grp)
        v = v_ref[0, r, :]
        attn = jnp.where(causal, _dot_nt(qe_ref[0, r, :], kn_ref[0, r, :]), 0.0).astype(BF16)
        oi_ref[r, :] = _dot(attn, v)
        for i in range(gi * GLA_GROUP, (gi + 1) * GLA_GROUP):
            rc = slice(i * c, (i + 1) * c)
            u_ref[i] = _dot_tn(kd_ref[0, rc, :], v_ref[0, rc, :])

    gn = gn_ref[...]
    decay_t = dl_ref[0].T
    state = jnp.zeros((dk, dv), F32)
    for i in range(n_chunks):
        rc = slice(i * c, (i + 1) * c)
        o = oi_ref[rc, :] + _dot(qe_ref[0, rc, :], state.astype(BF16))
        y = _rms(o, gn) * _silu(g_ref[0, rc, :].astype(F32))
        o_ref[0, rc, :] = y.astype(BF16)
        state = decay_t[:, i:i + 1] * state + u_ref[i]


def _gla_mix(qe, kn, kd, v, g, dlast, gn):
    b, s, key_dim = qe.shape
    val_dim = v.shape[2]
    dk, dv = key_dim // GLA_HEADS, val_dim // GLA_HEADS
    n_chunks = s // GLA_CHUNK
    kspec = pl.BlockSpec((1, s, dk), lambda i, j: (i, 0, j))
    vspec = pl.BlockSpec((1, s, dv), lambda i, j: (i, 0, j))
    return pl.pallas_call(
        _gla_mix_body,
        grid=(b, GLA_HEADS),
        in_specs=[kspec, kspec, kspec, vspec, vspec,
                  pl.BlockSpec((1, n_chunks, dk), lambda i, j: (i, 0, j)), _full((1, dv))],
        out_specs=vspec,
        out_shape=jax.ShapeDtypeStruct((b, s, val_dim), BF16),
        scratch_shapes=[pltpu.VMEM((s, dv), F32), pltpu.VMEM((n_chunks, dk, dv), F32)],
        compiler_params=_params("parallel", "parallel"),
        name="gla_mix",
    )(qe, kn, kd, v, g, dlast, gn.reshape(1, dv))


def _rope(x, cos, sin_signed):
    return x * cos + pltpu.roll(x, 64, 1) * sin_signed


def _mla_in_body(*refs, scale, fused):
    x, refs = _layer_input(refs, fused)
    (ng_ref, wa_ref, wg_ref, gq_ref, gkv_ref, wqn_ref, wqr_ref, wkn_ref, wv_ref,
     cos_ref, sin_ref, qn_ref, qr_ref, kn_ref, kr_ref, v_ref, g_ref) = refs
    h = _rms(x, ng_ref[...]).astype(BF16)
    pa = _dot(h, wa_ref[...])
    g_ref[...] = _dot(h, wg_ref[...]).astype(BF16)
    cos, sin = cos_ref[...], sin_ref[...]
    c_q = _rms(pa[:, :MLA_Q_RANK], gq_ref[...]).astype(BF16)
    c_kv = _rms(pa[:, MLA_Q_RANK:MLA_Q_RANK + MLA_KV_RANK], gkv_ref[...]).astype(BF16)
    kr_ref[...] = _rope(pa[:, MLA_Q_RANK + MLA_KV_RANK:], cos, sin).astype(BF16)
    qn_ref[...] = (_dot(c_q, wqn_ref[...]) * scale).astype(BF16)
    qr = _dot(c_q, wqr_ref[...])
    for p in range(qr.shape[1] // LANES):
        sl = slice(p * LANES, (p + 1) * LANES)
        qr_ref[:, sl] = (_rope(qr[:, sl], cos, sin) * scale).astype(BF16)
    kn_ref[...] = _dot(c_kv, wkn_ref[...]).astype(BF16)
    v_ref[...] = _dot(c_kv, wv_ref[...]).astype(BF16)


def _mla_in(x2, prev, ng, w_in, gq, gkv, w_uq, w_ukv, seq):
    t, d = x2.shape
    tm = ROW_TILE
    hq, half = MLA_HEADS, MLA_ROPE // 2
    o_kv, o_kr, o_g = MLA_Q_RANK, MLA_Q_RANK + MLA_KV_RANK, MLA_Q_RANK + MLA_KV_RANK + MLA_ROPE
    kr1, kr2 = w_in[:, o_kr:o_kr + half], w_in[:, o_kr + half:o_g]
    wa = jnp.concatenate([w_in[:, :o_kr], kr1, kr1, kr2, kr2], axis=1).astype(BF16)
    wg = w_in[:, o_g:].astype(BF16)
    uq = w_uq.reshape(MLA_Q_RANK, hq, MLA_NOPE + MLA_ROPE)
    wqn = uq[:, :, :MLA_NOPE].reshape(MLA_Q_RANK, hq * MLA_NOPE).astype(BF16)
    x1 = uq[:, :, MLA_NOPE:MLA_NOPE + half].reshape(MLA_Q_RANK, hq // 2, 2 * half)
    x2r = uq[:, :, MLA_NOPE + half:].reshape(MLA_Q_RANK, hq // 2, 2 * half)
    wqr = jnp.concatenate([x1, x2r], axis=2).reshape(MLA_Q_RANK, hq * MLA_ROPE).astype(BF16)
    ukv = w_ukv.reshape(MLA_KV_RANK, hq, MLA_NOPE + MLA_V)
    wkn = ukv[:, :, :MLA_NOPE].reshape(MLA_KV_RANK, hq * MLA_NOPE).astype(BF16)
    wv = ukv[:, :, MLA_NOPE:].reshape(MLA_KV_RANK, hq * MLA_V).astype(BF16)

    pos = jnp.arange(seq, dtype=F32)
    inv = ROPE_THETA ** (-jnp.arange(0, MLA_ROPE, 2, dtype=F32) / MLA_ROPE)
    ang = pos[:, None] * inv[None, :]
    cos, sin = jnp.cos(ang), jnp.sin(ang)
    cos4 = jnp.concatenate([cos, cos, cos, cos], axis=1)
    sin4 = jnp.concatenate([-sin, -sin, sin, sin], axis=1)

    scale = (MLA_NOPE + MLA_ROPE) ** -0.5
    tab = pl.BlockSpec((tm, LANES), lambda i: (i % (seq // tm), 0))
    wide = hq * MLA_NOPE
    arrays = [ng.reshape(1, d), wa, wg, gq.reshape(1, -1), gkv.reshape(1, -1), wqn, wqr, wkn, wv]
    widths = [wide, hq * MLA_ROPE, wide, LANES, wide, wide]
    return _in_call(
        functools.partial(_mla_in_body, scale=scale), "mla_in", x2, prev, arrays + [cos4, sin4],
        [_full(a.shape) for a in arrays] + [tab, tab],
        [_row(n) for n in widths], [jax.ShapeDtypeStruct((t, n), BF16) for n in widths])


def _mla_attn_body(qn_ref, qr_ref, kn_ref, kr_ref, v_ref, g_ref, o_ref, qc_ref, kc_ref, v1_ref):
    s = qn_ref.shape[1]
    t = MLA_TILE
    n_tiles = s // t
    lane = lax.broadcasted_iota(jnp.int32, (s, LANES), 1)
    for hh in range(MLA_STEP_HEADS):
        hl = slice(hh * LANES, (hh + 1) * LANES)
        mine = ((lane // (MLA_ROPE // 2)) % 2) == hh % 2
        rope = qr_ref[0, :, (hh // 2) * LANES:(hh // 2 + 1) * LANES]
        qc_ref[hh, :, :LANES] = qn_ref[0, :, hl]
        qc_ref[hh, :, LANES:] = jnp.where(mine, rope, jnp.zeros((), BF16))
        kc_ref[hh, :, :LANES] = kn_ref[0, :, hl]
        kc_ref[hh, :, LANES:] = kr_ref[0]
        v1_ref[hh, :, :LANES] = v_ref[0, :, hl]
        v1_ref[hh, :, LANES:] = jnp.ones((s, LANES), BF16)
    ri = lax.broadcasted_iota(jnp.int32, (t, t), 0)
    ci = lax.broadcasted_iota(jnp.int32, (t, t), 1)
    causal = ri >= ci

    for i in range(n_tiles):
        rq = slice(i * t, (i + 1) * t)
        for hh in range(MLA_STEP_HEADS):
            hl = slice(hh * LANES, (hh + 1) * LANES)
            q = qc_ref[hh, rq, :]
            s_diag = jnp.where(causal, _dot_nt(q, kc_ref[hh, rq, :]), -jnp.inf)
            m = jnp.max(s_diag, axis=-1, keepdims=True)
            if i > 0:
                s_past = _dot_nt(q, kc_ref[hh, :i * t, :])
                m = jnp.maximum(m, jnp.max(s_past, axis=-1, keepdims=True))
                acc = _dot(jnp.exp(s_past - m).astype(BF16), v1_ref[hh, :i * t, :])
            else:
                acc = jnp.zeros((t, 2 * LANES), F32)
            acc = acc + _dot(jnp.exp(s_diag - m).astype(BF16), v1_ref[hh, rq, :])
            o = acc[:, :LANES] / acc[:, LANES:]
            o_ref[0, rq, hl] = (o * _silu(g_ref[0, rq, hl].astype(F32))).astype(BF16)


def _mla_attn(qn, qr, kn, kr, v, g):
    b, s, wide = qn.shape
    nh = MLA_STEP_HEADS
    heads = pl.BlockSpec((1, s, nh * LANES), lambda i, j: (i, 0, j))
    return pl.pallas_call(
        _mla_attn_body,
        grid=(b, MLA_HEADS // nh),
        in_specs=[heads, pl.BlockSpec((1, s, nh // 2 * LANES), lambda i, j: (i, 0, j)), heads,
                  pl.BlockSpec((1, s, LANES), lambda i, j: (i, 0, 0)), heads, heads],
        out_specs=heads,
        out_shape=jax.ShapeDtypeStruct((b, s, wide), BF16),
        scratch_shapes=[pltpu.VMEM((nh, s, 2 * LANES), BF16)] * 3,
        compiler_params=_params("parallel", "parallel"),
        name="mla_attn",
    )(qn, qr, kn, kr, v, g)


def _sb_in_body(*refs, scale, fused):
    x, refs = _layer_input(refs, fused)
    ng_ref, wq_ref, wk_ref, wv_ref, wg_ref, q_ref, k_ref, v_ref, g_ref = refs
    h = _rms(x, ng_ref[...]).astype(BF16)
    q_ref[...] = (_dot(h, wq_ref[...]) * scale).astype(BF16)
    k_ref[...] = _dot(h, wk_ref[...]).astype(BF16)
    v_ref[...] = _dot(h, wv_ref[...]).astype(BF16)
    g_ref[...] = _dot(h, wg_ref[...]).astype(BF16)


def _sb_in(x2, prev, ng, w_in):
    t, d = x2.shape
    width = w_in.shape[1] // 4
    arrays = [ng.reshape(1, d)] + [w_in[:, i * width:(i + 1) * width].astype(BF16) for i in range(4)]
    return _in_call(
        functools.partial(_sb_in_body, scale=SB_HEAD_DIM ** -0.5), "sb_in", x2, prev, arrays,
        [_full(a.shape) for a in arrays], [_row(width)] * 4, [jax.ShapeDtypeStruct((t, width), BF16)] * 4)


def _sb_attn_body(q_ref, k_ref, v_ref, g_ref, m2_ref, o_ref, kp_ref, vp_ref):
    s = q_ref.shape[1]
    t = SB_TILE
    n_tiles = s // t
    ri = lax.broadcasted_iota(jnp.int32, (t, t), 0)
    ci = lax.broadcasted_iota(jnp.int32, (t, t), 1)
    strict = ri > ci
    lane = lax.broadcasted_iota(jnp.int32, (t, LANES), 1)
    first = lane < SB_HEAD_DIM
    m2 = m2_ref[...]

    def stats(z, masked):
        log_beta = _log_sigmoid(z)
        log_1m = log_beta - z
        if masked:
            log_1m = jnp.where(strict, log_1m, 0.0)
        hi = pltpu.bitcast(pltpu.bitcast(log_1m, jnp.uint32) & jnp.uint32(0xFFFF0000), F32)
        lo = (log_1m - hi).astype(BF16)
        res = _dot(jnp.concatenate([hi.astype(BF16), lo], axis=1), m2)
        return log_beta, res[:, :t], res[:, t:]

    def weights(log_beta, after, carry, masked):
        a = jnp.exp(log_beta + after + carry)
        if masked:
            a = jnp.where(strict, a, 0.0)
        return a.astype(BF16)

    pad = (SB_BAND - 1) * t
    kp_ref[:pad, :] = jnp.zeros((pad, LANES), BF16)
    vp_ref[:pad, :] = jnp.zeros((pad, LANES), BF16)
    kp_ref[pad:, :] = k_ref[0]
    vp_ref[pad:, :] = v_ref[0]

    def band(q2, i):
        rk = pl.ds(pl.multiple_of(i * t, t), SB_BAND * t)
        k, v = kp_ref[rk, :], vp_ref[rk, :]
        z2 = _dot_nt(q2, k)
        out = []
        for hh in range(2):
            zs = z2[hh * t:(hh + 1) * t]
            parts = [stats(zs[:, u * t:(u + 1) * t], u == SB_BAND - 1) for u in range(SB_BAND)]
            carry = jnp.zeros((t, t), F32)
            a_parts = [None] * SB_BAND
            for u in reversed(range(SB_BAND)):
                log_beta, after, row_sum = parts[u]
                a_parts[u] = weights(log_beta, after, carry, u == SB_BAND - 1)
                carry = carry + row_sum
            out += [carry, _dot(jnp.concatenate(a_parts, axis=1), v)]
        return tuple(out)

    def tail(q2, j0, st):
        def cond(c):
            return (c[0] >= 0) & (jnp.max(jnp.maximum(c[1], c[3])) > -SB_CUTOFF)

        def body(c):
            j = c[0]
            rk = pl.ds(pl.multiple_of(j * t + pad, t), t)
            k, v = kp_ref[rk, :], vp_ref[rk, :]
            z2 = _dot_nt(q2, k)
            new = []
            for hh in range(2):
                carry, acc = c[1 + 2 * hh], c[2 + 2 * hh]
                log_beta, after, row_sum = stats(z2[hh * t:(hh + 1) * t], False)
                new += [carry + row_sum, acc + _dot(weights(log_beta, after, carry, False), v)]
            return (j - 1, *new)

        return lax.while_loop(cond, body, (j0, *st))[1:]

    def masked_queries(rq):
        q = q_ref[0, rq, :]
        zero = jnp.zeros((), BF16)
        return jnp.concatenate([jnp.where(first, q, zero), jnp.where(first, zero, q)], axis=0)

    def finish(rq, st):
        o = jnp.where(first, st[1], st[3])
        o_ref[0, rq, :] = (o * _silu(g_ref[0, rq, :].astype(F32))).astype(BF16)

    def group(gi, c0):
        tiles = []
        worst = None
        for u in range(SB_GROUP):
            i = gi * SB_GROUP + u
            rq = pl.ds(pl.multiple_of(i * t, t), t)
            q2 = masked_queries(rq)
            st = band(q2, i)
            finish(rq, st)
            tiles.append((i, rq, q2, st))
            c = jnp.maximum(st[0], st[2])
            worst = c if worst is None else jnp.maximum(worst, c)

        @pl.when(jnp.max(worst) > -SB_CUTOFF)
        def _():
            for i, rq, q2, st in tiles:
                finish(rq, tail(q2, i - SB_BAND, st))

        return c0

    lax.fori_loop(0, n_tiles // SB_GROUP, group, 0)


def _sb_attn(q, k, v, g):
    b, s, width = q.shape
    t = SB_TILE
    assert s % (SB_GROUP * t) == 0, s
    r = jnp.arange(2 * t)
    m2 = (((r[:, None] % t) > r[None, :]) | (r[None, :] >= t)).astype(BF16)
    spec = pl.BlockSpec((1, s, LANES), lambda i, j: (i, 0, j))
    return pl.pallas_call(
        _sb_attn_body,
        grid=(b, width // LANES),
        in_specs=[spec, spec, spec, spec, _full((2 * t, 2 * t))],
        out_specs=spec,
        out_shape=jax.ShapeDtypeStruct((b, s, width), BF16),
        scratch_shapes=[pltpu.VMEM((s + (SB_BAND - 1) * t, LANES), BF16)] * 2,
        compiler_params=_params("parallel", "parallel"),
        name="sb_attn",
    )(q, k, v, g, m2)


def _out_body(x_ref, a_ref, w_ref, fg_ref, o_ref):
    o_ref[...] = _rms(x_ref[...] + _dot(a_ref[...], w_ref[...]), fg_ref[...])


def _out_proj(x2, a2, w_out, final_g):
    t, d = x2.shape
    kdim = a2.shape[1]
    tm = 2 * ROW_TILE
    row = lambda n: pl.BlockSpec((tm, n), lambda i: (i, 0))
    return pl.pallas_call(
        _out_body,
        grid=(t // tm,),
        in_specs=[row(d), row(kdim), _full((kdim, d)), _full((1, d))],
        out_specs=row(d),
        out_shape=jax.ShapeDtypeStruct((t, d), F32),
        compiler_params=_params("parallel"),
        name="out_proj",
    )(x2, a2, w_out.astype(BF16), final_g.reshape(1, d))


def kernel(x, norm_g, final_g, gla_w_in, gla_w_gate2, gla_b_gate, gla_norm_g, gla_w_out,
           mla_w_in, mla_q_norm_g, mla_kv_norm_g, mla_w_uq, mla_w_ukv, mla_w_out,
           sb_w_in, sb_w_out):
    b, s, d = x.shape
    depth = norm_g.shape[0]
    assert s % max(2 * ROW_TILE, MLA_TILE, SB_TILE, GLA_CHUNK) == 0, (b, s, d)
    t = b * s
    x2 = x.reshape(t, d)
    sh = lambda a: a.reshape(b, s, a.shape[-1])
    prev = None
    for layer in range(depth):
        kind, j = layer % 3, layer // 3
        if kind == 0:
            (qe, kn, kd, v, g, dl), x2 = _gla_in(x2, prev, norm_g[layer], gla_w_in[j], gla_w_gate2[j],
                                                 gla_b_gate[j])
            y = _gla_mix(sh(qe), sh(kn), sh(kd), sh(v), sh(g), dl.reshape(b, s // GLA_CHUNK, -1), gla_norm_g[j])
            w_out = gla_w_out[j]
        elif kind == 1:
            parts, x2 = _mla_in(x2, prev, norm_g[layer], mla_w_in[j], mla_q_norm_g[j], mla_kv_norm_g[j],
                                mla_w_uq[j], mla_w_ukv[j], s)
            y = _mla_attn(*[sh(a) for a in parts])
            w_out = mla_w_out[j]
        else:
            parts, x2 = _sb_in(x2, prev, norm_g[layer], sb_w_in[j])
            y = _sb_attn(*[sh(a) for a in parts])
            w_out = sb_w_out[j]
        prev = (y.reshape(t, -1), w_out)
    return _out_proj(x2, prev[0], prev[1], final_g).reshape(b, s, d)
```

```python
import functools
import math

import jax
import jax.numpy as jnp
from jax import lax
from jax.experimental import pallas as pl
from jax.experimental.pallas import tpu as pltpu

EPS = 1e-6
F32 = jnp.float32
BF16 = jnp.bfloat16

LANES = 128
VMEM_LIMIT_BYTES = 56 * 1024 * 1024

GLA_HEADS = 4
GLA_CHUNK = 64
GLA_GATE_RANK = 16
GLA_GATE_NORMALIZER = 16.0
GLA_GROUP = 4
GLA_IN_PARTS = 2
GLA_STEP_HEADS = 2
MLA_HEADS = 16
MLA_Q_RANK = 256
MLA_KV_RANK = 128
MLA_NOPE = 128
MLA_ROPE = 64
MLA_V = 128
ROPE_THETA = 10000.0
SB_HEADS = 16
SB_HEAD_DIM = 64

ROW_TILE = 256
MLA_TILE = 256
MLA_STEP_HEADS = 4
SB_TILE = 128
SB_BAND = 3
SB_GROUP = 8
SB_CUTOFF = 120.0


def _dot(a, b):
    return jnp.dot(a, b, preferred_element_type=F32)


def _dot_nt(a, b):
    return lax.dot_general(a, b, (((1,), (1,)), ((), ())), preferred_element_type=F32)


def _dot_tn(a, b):
    return lax.dot_general(a, b, (((0,), (0,)), ((), ())), preferred_element_type=F32)


def _rms(x, g):
    return x * lax.rsqrt(jnp.mean(x * x, axis=-1, keepdims=True) + EPS) * g


def _log_sigmoid(z):
    return jnp.minimum(z, 0.0) - jnp.log(1.0 + jnp.exp(-jnp.abs(z)))


def _silu(g):
    return g * (1.0 / (1.0 + jnp.exp(-g)))


def _params(*sem):
    return pltpu.CompilerParams(dimension_semantics=sem, vmem_limit_bytes=VMEM_LIMIT_BYTES)


def _full(shape):
    return pl.BlockSpec(shape, lambda *_: (0,) * len(shape))


def _row(n, tm=ROW_TILE):
    return pl.BlockSpec((tm, n), lambda i: (i, 0))


def _layer_input(refs, fused):
    if not fused:
        return refs[0][...], refs[1:]
    x_ref, a_ref, wo_ref, *rest, xo_ref = refs
    x = x_ref[...] + _dot(a_ref[...], wo_ref[...])
    xo_ref[...] = x
    return x, rest


def _in_call(body, name, x2, prev, arrays, specs, out_specs, out_shape, tm=ROW_TILE):
    t, d = x2.shape
    fused = prev is not None
    ins, in_specs = [x2], [_row(d, tm)]
    if fused:
        a2, w_out = prev
        ins += [a2, w_out.astype(BF16)]
        in_specs += [_row(a2.shape[1], tm), _full(w_out.shape)]
        out_specs = out_specs + [_row(d, tm)]
        out_shape = out_shape + [jax.ShapeDtypeStruct((t, d), F32)]
    outs = pl.pallas_call(
        functools.partial(body, fused=fused),
        grid=(t // tm,),
        in_specs=in_specs + specs,
        out_specs=out_specs,
        out_shape=out_shape,
        compiler_params=_params("parallel"),
        name=name,
    )(*ins, *arrays)
    return (outs[:-1], outs[-1]) if fused else (outs, x2)


def _gla_in_body(*refs, q_scale, fused):
    x, refs = _layer_input(refs, fused)
    (ng_ref, wq_ref, wk_ref, wv_ref, wg_ref, wa_ref, w2_ref, bg_ref, tri_ref,
     qe_ref, kn_ref, kd_ref, v_ref, g_ref, dl_ref) = refs
    tri = tri_ref[...]
    rows = tri.shape[0]
    for part in range(x.shape[0] // rows):
        r = slice(part * rows, (part + 1) * rows)
        h = _rms(x[r], ng_ref[...]).astype(BF16)
        q = _dot(h, wq_ref[...]) * q_scale
        k = _dot(h, wk_ref[...])
        v_ref[r, :] = _dot(h, wv_ref[...]).astype(BF16)
        g_ref[r, :] = _dot(h, wg_ref[...]).astype(BF16)
        a_low = _dot(h, wa_ref[...]).astype(BF16)
        log_a = _log_sigmoid(_dot(a_low, w2_ref[...]) + bg_ref[...]) * (1.0 / GLA_GATE_NORMALIZER)
        hi = log_a.astype(BF16)
        lo = (log_a - hi.astype(F32)).astype(BF16)
        b = _dot(tri, hi) + _dot(tri, lo)
        dk = b.shape[1]
        n = rows // GLA_CHUNK
        b3 = b.reshape(n, GLA_CHUNK, dk)
        b_last = b3[:, GLA_CHUNK - 1:GLA_CHUNK, :]
        qe_ref[r, :] = (q * jnp.exp(b)).astype(BF16)
        kn_ref[r, :] = (k * jnp.exp(-b)).astype(BF16)
        k3 = k.reshape(n, GLA_CHUNK, dk)
        kd_ref[r, :] = (k3 * jnp.exp(b_last - b3)).reshape(rows, dk).astype(BF16)
        dl_ref[0, part * n:(part + 1) * n, :] = jnp.exp(b_last).reshape(n, dk)


def _gla_in(x2, prev, ng, w_in, w_gate2, b_gate):
    t, d = x2.shape
    key_dim = w_gate2.shape[1]
    val_dim = (w_in.shape[1] - 2 * key_dim - GLA_GATE_RANK) // 2
    dk = key_dim // GLA_HEADS
    tm = GLA_IN_PARTS * ROW_TILE
    wq = w_in[:, :key_dim].astype(BF16)
    wk = w_in[:, key_dim:2 * key_dim].astype(BF16)
    wv = w_in[:, 2 * key_dim:2 * key_dim + val_dim].astype(BF16)
    wg = w_in[:, 2 * key_dim + val_dim:2 * key_dim + 2 * val_dim].astype(BF16)
    wa = jnp.pad(w_in[:, 2 * key_dim + 2 * val_dim:], ((0, 0), (0, LANES - GLA_GATE_RANK))).astype(BF16)
    w2 = jnp.pad(w_gate2, ((0, LANES - GLA_GATE_RANK), (0, 0))).astype(BF16)
    r = jnp.arange(ROW_TILE)
    tri = ((r[:, None] // GLA_CHUNK == r[None, :] // GLA_CHUNK) & (r[:, None] >= r[None, :])).astype(BF16)
    arrays = [ng.reshape(1, d), wq, wk, wv, wg, wa, w2, b_gate.reshape(1, key_dim), tri]
    return _in_call(
        functools.partial(_gla_in_body, q_scale=dk ** -0.5), "gla_in", x2, prev, arrays,
        [_full(a.shape) for a in arrays],
        [_row(key_dim, tm), _row(key_dim, tm), _row(key_dim, tm), _row(val_dim, tm), _row(val_dim, tm),
         pl.BlockSpec((1, tm // GLA_CHUNK, key_dim), lambda i: (i, 0, 0))],
        [jax.ShapeDtypeStruct((t, key_dim), BF16)] * 3 + [jax.ShapeDtypeStruct((t, val_dim), BF16)] * 2
        + [jax.ShapeDtypeStruct((t // tm, tm // GLA_CHUNK, key_dim), F32)], tm=tm)


def _gla_mix_body(qe_ref, kn_ref, kd_ref, v_ref, g_ref, dl_ref, gn_ref, o_ref, oi_ref, u_ref):
    n_chunks = dl_ref.shape[1]
    c = GLA_CHUNK
    grp = GLA_GROUP * c
    ri = lax.broadcasted_iota(jnp.int32, (grp, grp), 0)
    ci = lax.broadcasted_iota(jnp.int32, (grp, grp), 1)
    causal = (ri // c == ci // c) & (ri >= ci)
    nh, dk, dv = u_ref.shape[0], u_ref.shape[2], u_ref.shape[3]
    kl = [slice(hh * dk, (hh + 1) * dk) for hh in range(nh)]
    vl = [slice(hh * dv, (hh + 1) * dv) for hh in range(nh)]

    for gi in range(n_chunks // GLA_GROUP):
        r = slice(gi * grp, (gi + 1) * grp)
        for hh in range(nh):
            attn = jnp.where(causal, _dot_nt(qe_ref[0, r, kl[hh]], kn_ref[0, r, kl[hh]]), 0.0).astype(BF16)
            oi_ref[hh, r, :] = _dot(attn, v_ref[0, r, vl[hh]])
            for i in range(gi * GLA_GROUP, (gi + 1) * GLA_GROUP):
                rc = slice(i * c, (i + 1) * c)
                u_ref[hh, i] = _dot_tn(kd_ref[0, rc, kl[hh]], v_ref[0, rc, vl[hh]])

    gn = gn_ref[...]
    decay_t = dl_ref[0].T
    states = [jnp.zeros((dk, dv), F32) for _ in range(nh)]
    for i in range(n_chunks):
        rc = slice(i * c, (i + 1) * c)
        for hh in range(nh):
            o = oi_ref[hh, rc, :] + _dot(qe_ref[0, rc, kl[hh]], states[hh].astype(BF16))
            y = _rms(o, gn) * _silu(g_ref[0, rc, vl[hh]].astype(F32))
            o_ref[0, rc, vl[hh]] = y.astype(BF16)
            states[hh] = decay_t[kl[hh], i:i + 1] * states[hh] + u_ref[hh, i]


def _gla_mix(qe, kn, kd, v, g, dlast, gn):
    b, s, key_dim = qe.shape
    val_dim = v.shape[2]
    dk, dv = key_dim // GLA_HEADS, val_dim // GLA_HEADS
    n_chunks = s // GLA_CHUNK
    nh = GLA_STEP_HEADS
    kspec = pl.BlockSpec((1, s, nh * dk), lambda i, j: (i, 0, j))
    vspec = pl.BlockSpec((1, s, nh * dv), lambda i, j: (i, 0, j))
    return pl.pallas_call(
        _gla_mix_body,
        grid=(b, GLA_HEADS // nh),
        in_specs=[kspec, kspec, kspec, vspec, vspec,
                  pl.BlockSpec((1, n_chunks, nh * dk), lambda i, j: (i, 0, j)), _full((1, dv))],
        out_specs=vspec,
        out_shape=jax.ShapeDtypeStruct((b, s, val_dim), BF16),
        scratch_shapes=[pltpu.VMEM((nh, s, dv), F32), pltpu.VMEM((nh, n_chunks, dk, dv), F32)],
        compiler_params=_params("parallel", "parallel"),
        name="gla_mix",
    )(qe, kn, kd, v, g, dlast, gn.reshape(1, dv))


def _rope(x, cos, sin_signed):
    return x * cos + pltpu.roll(x, 64, 1) * sin_signed


def _mla_in_body(*refs, scale, fused):
    x, refs = _layer_input(refs, fused)
    (ng_ref, wa_ref, wg_ref, gq_ref, gkv_ref, wqn_ref, wqr_ref, wkn_ref, wv_ref,
     cos_ref, sin_ref, qn_ref, qr_ref, kn_ref, kr_ref, v_ref, g_ref) = refs
    h = _rms(x, ng_ref[...]).astype(BF16)
    pa = _dot(h, wa_ref[...])
    g_ref[...] = _dot(h, wg_ref[...]).astype(BF16)
    cos, sin = cos_ref[...], sin_ref[...]
    c_q = _rms(pa[:, :MLA_Q_RANK], gq_ref[...]).astype(BF16)
    c_kv = _rms(pa[:, MLA_Q_RANK:MLA_Q_RANK + MLA_KV_RANK], gkv_ref[...]).astype(BF16)
    kr_ref[...] = _rope(pa[:, MLA_Q_RANK + MLA_KV_RANK:], cos, sin).astype(BF16)
    qn_ref[...] = (_dot(c_q, wqn_ref[...]) * scale).astype(BF16)
    qr = _dot(c_q, wqr_ref[...])
    for p in range(qr.shape[1] // LANES):
        sl = slice(p * LANES, (p + 1) * LANES)
        qr_ref[:, sl] = (_rope(qr[:, sl], cos, sin) * scale).astype(BF16)
    kn_ref[...] = _dot(c_kv, wkn_ref[...]).astype(BF16)
    v_ref[...] = _dot(c_kv, wv_ref[...]).astype(BF16)


def _mla_in(x2, prev, ng, w_in, gq, gkv, w_uq, w_ukv, seq):
    t, d = x2.shape
    tm = ROW_TILE
    hq, half = MLA_HEADS, MLA_ROPE // 2
    o_kv, o_kr, o_g = MLA_Q_RANK, MLA_Q_RANK + MLA_KV_RANK, MLA_Q_RANK + MLA_KV_RANK + MLA_ROPE
    kr1, kr2 = w_in[:, o_kr:o_kr + half], w_in[:, o_kr + half:o_g]
    wa = jnp.concatenate([w_in[:, :o_kr], kr1, kr1, kr2, kr2], axis=1).astype(BF16)
    wg = w_in[:, o_g:].astype(BF16)
    uq = w_uq.reshape(MLA_Q_RANK, hq, MLA_NOPE + MLA_ROPE)
    wqn = uq[:, :, :MLA_NOPE].reshape(MLA_Q_RANK, hq * MLA_NOPE).astype(BF16)
    x1 = uq[:, :, MLA_NOPE:MLA_NOPE + half].reshape(MLA_Q_RANK, hq // 2, 2 * half)
    x2r = uq[:, :, MLA_NOPE + half:].reshape(MLA_Q_RANK, hq // 2, 2 * half)
    wqr = jnp.concatenate([x1, x2r], axis=2).reshape(MLA_Q_RANK, hq * MLA_ROPE).astype(BF16)
    ukv = w_ukv.reshape(MLA_KV_RANK, hq, MLA_NOPE + MLA_V)
    wkn = ukv[:, :, :MLA_NOPE].reshape(MLA_KV_RANK, hq * MLA_NOPE).astype(BF16)
    wv = ukv[:, :, MLA_NOPE:].reshape(MLA_KV_RANK, hq * MLA_V).astype(BF16)

    pos = jnp.arange(seq, dtype=F32)
    inv = ROPE_THETA ** (-jnp.arange(0, MLA_ROPE, 2, dtype=F32) / MLA_ROPE)
    ang = pos[:, None] * inv[None, :]
    cos, sin = jnp.cos(ang), jnp.sin(ang)
    cos4 = jnp.concatenate([cos, cos, cos, cos], axis=1)
    sin4 = jnp.concatenate([-sin, -sin, sin, sin], axis=1)

    scale = (MLA_NOPE + MLA_ROPE) ** -0.5
    tab = pl.BlockSpec((tm, LANES), lambda i: (i % (seq // tm), 0))
    wide = hq * MLA_NOPE
    arrays = [ng.reshape(1, d), wa, wg, gq.reshape(1, -1), gkv.reshape(1, -1), wqn, wqr, wkn, wv]
    widths = [wide, hq * MLA_ROPE, wide, LANES, wide, wide]
    return _in_call(
        functools.partial(_mla_in_body, scale=scale), "mla_in", x2, prev, arrays + [cos4, sin4],
        [_full(a.shape) for a in arrays] + [tab, tab],
        [_row(n) for n in widths], [jax.ShapeDtypeStruct((t, n), BF16) for n in widths])


def _mla_attn_body(qn_ref, qr_ref, kn_ref, kr_ref, v_ref, g_ref, o_ref, qc_ref, kc_ref, v1_ref):
    s = qn_ref.shape[1]
    t = MLA_TILE
    n_tiles = s // t
    lane = lax.broadcasted_iota(jnp.int32, (s, LANES), 1)
    for hh in range(MLA_STEP_HEADS):
        hl = slice(hh * LANES, (hh + 1) * LANES)
        mine = ((lane // (MLA_ROPE // 2)) % 2) == hh % 2
        rope = qr_ref[0, :, (hh // 2) * LANES:(hh // 2 + 1) * LANES]
        qc_ref[hh, :, :LANES] = qn_ref[0, :, hl]
        qc_ref[hh, :, LANES:] = jnp.where(mine, rope, jnp.zeros((), BF16))
        kc_ref[hh, :, :LANES] = kn_ref[0, :, hl]
        kc_ref[hh, :, LANES:] = kr_ref[0]
        v1_ref[hh, :, :LANES] = v_ref[0, :, hl]
        v1_ref[hh, :, LANES:] = jnp.ones((s, LANES), BF16)
    ri = lax.broadcasted_iota(jnp.int32, (t, t), 0)
    ci = lax.broadcasted_iota(jnp.int32, (t, t), 1)
    causal = ri >= ci

    for i in range(n_tiles):
        rq = slice(i * t, (i + 1) * t)
        for hh in range(MLA_STEP_HEADS):
            hl = slice(hh * LANES, (hh + 1) * LANES)
            q = qc_ref[hh, rq, :]
            s_diag = jnp.where(causal, _dot_nt(q, kc_ref[hh, rq, :]), -jnp.inf)
            m = jnp.max(s_diag, axis=-1, keepdims=True)
            if i > 0:
                s_past = _dot_nt(q, kc_ref[hh, :i * t, :])
                m = jnp.maximum(m, jnp.max(s_past, axis=-1, keepdims=True))
                acc = _dot(jnp.exp(s_past - m).astype(BF16), v1_ref[hh, :i * t, :])
            else:
                acc = jnp.zeros((t, 2 * LANES), F32)
            acc = acc + _dot(jnp.exp(s_diag - m).astype(BF16), v1_ref[hh, rq, :])
            o = acc[:, :LANES] / acc[:, LANES:]
            o_ref[0, rq, hl] = (o * _silu(g_ref[0, rq, hl].astype(F32))).astype(BF16)


def _mla_attn(qn, qr, kn, kr, v, g):
    b, s, wide = qn.shape
    nh = MLA_STEP_HEADS
    heads = pl.BlockSpec((1, s, nh * LANES), lambda i, j: (i, 0, j))
    return pl.pallas_call(
        _mla_attn_body,
        grid=(b, MLA_HEADS // nh),
        in_specs=[heads, pl.BlockSpec((1, s, nh // 2 * LANES), lambda i, j: (i, 0, j)), heads,
                  pl.BlockSpec((1, s, LANES), lambda i, j: (i, 0, 0)), heads, heads],
        out_specs=heads,
        out_shape=jax.ShapeDtypeStruct((b, s, wide), BF16),
        scratch_shapes=[pltpu.VMEM((nh, s, 2 * LANES), BF16)] * 3,
        compiler_params=_params("parallel", "parallel"),
        name="mla_attn",
    )(qn, qr, kn, kr, v, g)


def _sb_in_body(*refs, scale, fused):
    x, refs = _layer_input(refs, fused)
    ng_ref, wq_ref, wk_ref, wv_ref, wg_ref, q_ref, k_ref, v_ref, g_ref = refs
    h = _rms(x, ng_ref[...]).astype(BF16)
    q_ref[...] = (_dot(h, wq_ref[...]) * scale).astype(BF16)
    k_ref[...] = _dot(h, wk_ref[...]).astype(BF16)
    v_ref[...] = _dot(h, wv_ref[...]).astype(BF16)
    g_ref[...] = _dot(h, wg_ref[...]).astype(BF16)


def _sb_in(x2, prev, ng, w_in):
    t, d = x2.shape
    width = w_in.shape[1] // 4
    arrays = [ng.reshape(1, d)] + [w_in[:, i * width:(i + 1) * width].astype(BF16) for i in range(4)]
    return _in_call(
        functools.partial(_sb_in_body, scale=SB_HEAD_DIM ** -0.5), "sb_in", x2, prev, arrays,
        [_full(a.shape) for a in arrays], [_row(width)] * 4, [jax.ShapeDtypeStruct((t, width), BF16)] * 4)


def _sb_attn_body(q_ref, k_ref, v_ref, g_ref, m2_ref, o_ref, kp_ref, vp_ref):
    s = q_ref.shape[1]
    t = SB_TILE
    n_tiles = s // t
    ri = lax.broadcasted_iota(jnp.int32, (t, t), 0)
    ci = lax.broadcasted_iota(jnp.int32, (t, t), 1)
    strict = ri > ci
    lane = lax.broadcasted_iota(jnp.int32, (t, LANES), 1)
    first = lane < SB_HEAD_DIM
    m2 = m2_ref[...]

    def stats(z, masked):
        log_beta = _log_sigmoid(z)
        log_1m = log_beta - z
        if masked:
            log_1m = jnp.where(strict, log_1m, 0.0)
        hi = pltpu.bitcast(pltpu.bitcast(log_1m, jnp.uint32) & jnp.uint32(0xFFFF0000), F32)
        lo = (log_1m - hi).astype(BF16)
        res = _dot(jnp.concatenate([hi.astype(BF16), lo], axis=1), m2)
        return log_beta, res[:, :t], res[:, t:]

    def weights(log_beta, after, carry, masked):
        a = jnp.exp(log_beta + after + carry)
        if masked:
            a = jnp.where(strict, a, 0.0)
        return a.astype(BF16)

    pad = (SB_BAND - 1) * t
    kp_ref[:pad, :] = jnp.zeros((pad, LANES), BF16)
    vp_ref[:pad, :] = jnp.zeros((pad, LANES), BF16)
    kp_ref[pad:, :] = k_ref[0]
    vp_ref[pad:, :] = v_ref[0]

    def band(q2, i):
        rk = pl.ds(pl.multiple_of(i * t, t), SB_BAND * t)
        k, v = kp_ref[rk, :], vp_ref[rk, :]
        z2 = _dot_nt(q2, k)
        out = []
        for hh in range(2):
            zs = z2[hh * t:(hh + 1) * t]
            parts = [stats(zs[:, u * t:(u + 1) * t], u == SB_BAND - 1) for u in range(SB_BAND)]
            carry = jnp.zeros((t, t), F32)
            a_parts = [None] * SB_BAND
            for u in reversed(range(SB_BAND)):
                log_beta, after, row_sum = parts[u]
                a_parts[u] = weights(log_beta, after, carry, u == SB_BAND - 1)
                carry = carry + row_sum
            out += [carry, _dot(jnp.concatenate(a_parts, axis=1), v)]
        return tuple(out)

    def tail(q2, j0, st):
        def cond(c):
            return (c[0] >= 0) & (jnp.max(jnp.maximum(c[1], c[3])) > -SB_CUTOFF)

        def body(c):
            j = c[0]
            rk = pl.ds(pl.multiple_of(j * t + pad, t), t)
            k, v = kp_ref[rk, :], vp_ref[rk, :]
            z2 = _dot_nt(q2, k)
            new = []
            for hh in range(2):
                carry, acc = c[1 + 2 * hh], c[2 + 2 * hh]
                log_beta, after, row_sum = stats(z2[hh * t:(hh + 1) * t], False)
                new += [carry + row_sum, acc + _dot(weights(log_beta, after, carry, False), v)]
            return (j - 1, *new)

        return lax.while_loop(cond, body, (j0, *st))[1:]

    def masked_queries(rq):
        q = q_ref[0, rq, :]
        zero = jnp.zeros((), BF16)
        return jnp.concatenate([jnp.where(first, q, zero), jnp.where(first, zero, q)], axis=0)

    def finish(rq, st):
        o = jnp.where(first, st[1], st[3])
        o_ref[0, rq, :] = (o * _silu(g_ref[0, rq, :].astype(F32))).astype(BF16)

    def group(gi, c0):
        tiles = []
        worst = None
        for u in range(SB_GROUP):
            i = gi * SB_GROUP + u
            rq = pl.ds(pl.multiple_of(i * t, t), t)
            q2 = masked_queries(rq)
            st = band(q2, i)
            finish(rq, st)
            tiles.append((i, rq, q2, st))
            c = jnp.maximum(st[0], st[2])
            worst = c if worst is None else jnp.maximum(worst, c)

        @pl.when(jnp.max(worst) > -SB_CUTOFF)
        def _():
            for i, rq, q2, st in tiles:
                finish(rq, tail(q2, i - SB_BAND, st))

        return c0

    lax.fori_loop(0, n_tiles // SB_GROUP, group, 0)


def _sb_attn(q, k, v, g):
    b, s, width = q.shape
    t = SB_TILE
    assert s % (SB_GROUP * t) == 0, s
    r = jnp.arange(2 * t)
    m2 = (((r[:, None] % t) > r[None, :]) | (r[None, :] >= t)).astype(BF16)
    spec = pl.BlockSpec((1, s, LANES), lambda i, j: (i, 0, j))
    return pl.pallas_call(
        _sb_attn_body,
        grid=(b, width // LANES),
        in_specs=[spec, spec, spec, spec, _full((2 * t, 2 * t))],
        out_specs=spec,
        out_shape=jax.ShapeDtypeStruct((b, s, width), BF16),
        scratch_shapes=[pltpu.VMEM((s + (SB_BAND - 1) * t, LANES), BF16)] * 2,
        compiler_params=_params("parallel", "parallel"),
        name="sb_attn",
    )(q, k, v, g, m2)


def _out_body(x_ref, a_ref, w_ref, fg_ref, o_ref):
    o_ref[...] = _rms(x_ref[...] + _dot(a_ref[...], w_ref[...]), fg_ref[...])


def _out_proj(x2, a2, w_out, final_g):
    t, d = x2.shape
    kdim = a2.shape[1]
    tm = 2 * ROW_TILE
    row = lambda n: pl.BlockSpec((tm, n), lambda i: (i, 0))
    return pl.pallas_call(
        _out_body,
        grid=(t // tm,),
        in_specs=[row(d), row(kdim), _full((kdim, d)), _full((1, d))],
        out_specs=row(d),
        out_shape=jax.ShapeDtypeStruct((t, d), F32),
        compiler_params=_params("parallel"),
        name="out_proj",
    )(x2, a2, w_out.astype(BF16), final_g.reshape(1, d))


def kernel(x, norm_g, final_g, gla_w_in, gla_w_gate2, gla_b_gate, gla_norm_g, gla_w_out,
           mla_w_in, mla_q_norm_g, mla_kv_norm_g, mla_w_uq, mla_w_ukv, mla_w_out,
           sb_w_in, sb_w_out):
    b, s, d = x.shape
    depth = norm_g.shape[0]
    assert s % max(2 * ROW_TILE, MLA_TILE, SB_TILE, GLA_CHUNK) == 0, (b, s, d)
    t = b * s
    x2 = x.reshape(t, d)
    sh = lambda a: a.reshape(b, s, a.shape[-1])
    prev = None
    for layer in range(depth):
        kind, j = layer % 3, layer // 3
        if kind == 0:
            (qe, kn, kd, v, g, dl), x2 = _gla_in(x2, prev, norm_g[layer], gla_w_in[j], gla_w_gate2[j],
                                                 gla_b_gate[j])
            y = _gla_mix(sh(qe), sh(kn), sh(kd), sh(v), sh(g), dl.reshape(b, s // GLA_CHUNK, -1), gla_norm_g[j])
            w_out = gla_w_out[j]
        elif kind == 1:
            parts, x2 = _mla_in(x2, prev, norm_g[layer], mla_w_in[j], mla_q_norm_g[j], mla_kv_norm_g[j],
                                mla_w_uq[j], mla_w_ukv[j], s)
            y = _mla_attn(*[sh(a) for a in parts])
            w_out = mla_w_out[j]
        else:
            parts, x2 = _sb_in(x2, prev, norm_g[layer], sb_w_in[j])
            y = _sb_attn(*[sh(a) for a in parts])
            w_out = sb_w_out[j]
        prev = (y.reshape(t, -1), w_out)
    return _out_proj(x2, prev[0], prev[1], final_g).reshape(b, s, d)
```

```python
import functools
import math

import jax
import jax.numpy as jnp
from jax import lax
from jax.experimental import pallas as pl
from jax.experimental.pallas import tpu as pltpu

EPS = 1e-6
F32 = jnp.float32
BF16 = jnp.bfloat16

LANES = 128
VMEM_LIMIT_BYTES = 56 * 1024 * 1024

GLA_HEADS = 4
GLA_CHUNK = 64
GLA_GATE_RANK = 16
GLA_GATE_NORMALIZER = 16.0
GLA_GROUP = 4
GLA_IN_PARTS = 2
GLA_STEP_HEADS = 2
MLA_HEADS = 16
MLA_Q_RANK = 256
MLA_KV_RANK = 128
MLA_NOPE = 128
MLA_ROPE = 64
MLA_V = 128
ROPE_THETA = 10000.0
SB_HEADS = 16
SB_HEAD_DIM = 64

ROW_TILE = 256
MLA_TILE = 256
MLA_STEP_HEADS = 4
SB_TILE = 128
SB_BAND = 3
SB_GROUP = 8
SB_CUTOFF = 120.0


def _dot(a, b):
    return jnp.dot(a, b, preferred_element_type=F32)


def _dot_nt(a, b):
    return lax.dot_general(a, b, (((1,), (1,)), ((), ())), preferred_element_type=F32)


def _dot_tn(a, b):
    return lax.dot_general(a, b, (((0,), (0,)), ((), ())), preferred_element_type=F32)


def _rms(x, g):
    return x * lax.rsqrt(jnp.mean(x * x, axis=-1, keepdims=True) + EPS) * g


def _log_sigmoid(z):
    return jnp.minimum(z, 0.0) - jnp.log(1.0 + jnp.exp(-jnp.abs(z)))


def _silu(g):
    return g * (1.0 / (1.0 + jnp.exp(-g)))


def _params(*sem):
    return pltpu.CompilerParams(dimension_semantics=sem, vmem_limit_bytes=VMEM_LIMIT_BYTES)


def _full(shape):
    return pl.BlockSpec(shape, lambda *_: (0,) * len(shape))


def _row(n, tm=ROW_TILE):
    return pl.BlockSpec((tm, n), lambda i: (i, 0))


def _layer_input(refs, fused):
    if not fused:
        return refs[0][...], refs[1:]
    x_ref, a_ref, wo_ref, *rest, xo_ref = refs
    x = x_ref[...] + _dot(a_ref[...], wo_ref[...])
    xo_ref[...] = x
    return x, rest


def _in_call(body, name, x2, prev, arrays, specs, out_specs, out_shape, tm=ROW_TILE):
    t, d = x2.shape
    fused = prev is not None
    ins, in_specs = [x2], [_row(d, tm)]
    if fused:
        a2, w_out = prev
        ins += [a2, w_out.astype(BF16)]
        in_specs += [_row(a2.shape[1], tm), _full(w_out.shape)]
        out_specs = out_specs + [_row(d, tm)]
        out_shape = out_shape + [jax.ShapeDtypeStruct((t, d), F32)]
    outs = pl.pallas_call(
        functools.partial(body, fused=fused),
        grid=(t // tm,),
        in_specs=in_specs + specs,
        out_specs=out_specs,
        out_shape=out_shape,
        compiler_params=_params("parallel"),
        name=name,
    )(*ins, *arrays)
    return (outs[:-1], outs[-1]) if fused else (outs, x2)


def _gla_in_body(*refs, q_scale, fused):
    x, refs = _layer_input(refs, fused)
    (ng_ref, wq_ref, wk_ref, wv_ref, wg_ref, wa_ref, w2_ref, bg_ref, tri_ref,
     qe_ref, kn_ref, kd_ref, v_ref, g_ref, dl_ref) = refs
    tri = tri_ref[...]
    rows = tri.shape[0]
    for part in range(x.shape[0] // rows):
        r = slice(part * rows, (part + 1) * rows)
        h = _rms(x[r], ng_ref[...]).astype(BF16)
        q = _dot(h, wq_ref[...]) * q_scale
        k = _dot(h, wk_ref[...])
        v_ref[r, :] = _dot(h, wv_ref[...]).astype(BF16)
        g_ref[r, :] = _dot(h, wg_ref[...]).astype(BF16)
        a_low = _dot(h, wa_ref[...]).astype(BF16)
        log_a = _log_sigmoid(_dot(a_low, w2_ref[...]) + bg_ref[...]) * (1.0 / GLA_GATE_NORMALIZER)
        hi = log_a.astype(BF16)
        lo = (log_a - hi.astype(F32)).astype(BF16)
        b = _dot(tri, hi) + _dot(tri, lo)
        dk = b.shape[1]
        n = rows // GLA_CHUNK
        b3 = b.reshape(n, GLA_CHUNK, dk)
        b_last = b3[:, GLA_CHUNK - 1:GLA_CHUNK, :]
        qe_ref[r, :] = (q * jnp.exp(b)).astype(BF16)
        kn_ref[r, :] = (k * jnp.exp(-b)).astype(BF16)
        k3 = k.reshape(n, GLA_CHUNK, dk)
        kd_ref[r, :] = (k3 * jnp.exp(b_last - b3)).reshape(rows, dk).astype(BF16)
        dl_ref[0, part * n:(part + 1) * n, :] = jnp.exp(b_last).reshape(n, dk)


def _gla_in(x2, prev, ng, w_in, w_gate2, b_gate):
    t, d = x2.shape
    key_dim = w_gate2.shape[1]
    val_dim = (w_in.shape[1] - 2 * key_dim - GLA_GATE_RANK) // 2
    dk = key_dim // GLA_HEADS
    tm = GLA_IN_PARTS * ROW_TILE
    wq = w_in[:, :key_dim].astype(BF16)
    wk = w_in[:, key_dim:2 * key_dim].astype(BF16)
    wv = w_in[:, 2 * key_dim:2 * key_dim + val_dim].astype(BF16)
    wg = w_in[:, 2 * key_dim + val_dim:2 * key_dim + 2 * val_dim].astype(BF16)
    wa = jnp.pad(w_in[:, 2 * key_dim + 2 * val_dim:], ((0, 0), (0, LANES - GLA_GATE_RANK))).astype(BF16)
    w2 = jnp.pad(w_gate2, ((0, LANES - GLA_GATE_RANK), (0, 0))).astype(BF16)
    r = jnp.arange(ROW_TILE)
    tri = ((r[:, None] // GLA_CHUNK == r[None, :] // GLA_CHUNK) & (r[:, None] >= r[None, :])).astype(BF16)
    arrays = [ng.reshape(1, d), wq, wk, wv, wg, wa, w2, b_gate.reshape(1, key_dim), tri]
    return _in_call(
        functools.partial(_gla_in_body, q_scale=dk ** -0.5), "gla_in", x2, prev, arrays,
        [_full(a.shape) for a in arrays],
        [_row(key_dim, tm), _row(key_dim, tm), _row(key_dim, tm), _row(val_dim, tm), _row(val_dim, tm),
         pl.BlockSpec((1, tm // GLA_CHUNK, key_dim), lambda i: (i, 0, 0))],
        [jax.ShapeDtypeStruct((t, key_dim), BF16)] * 3 + [jax.ShapeDtypeStruct((t, val_dim), BF16)] * 2
        + [jax.ShapeDtypeStruct((t // tm, tm // GLA_CHUNK, key_dim), F32)], tm=tm)


def _gla_mix_body(qe_ref, kn_ref, kd_ref, v_ref, g_ref, dl_ref, gn_ref, o_ref, oi_ref, u_ref):
    n_chunks = dl_ref.shape[1]
    c = GLA_CHUNK
    grp = GLA_GROUP * c
    ri = lax.broadcasted_iota(jnp.int32, (grp, grp), 0)
    ci = lax.broadcasted_iota(jnp.int32, (grp, grp), 1)
    causal = (ri // c == ci // c) & (ri >= ci)
    nh, dk, dv = u_ref.shape[0], u_ref.shape[2], u_ref.shape[3]
    kl = [slice(hh * dk, (hh + 1) * dk) for hh in range(nh)]
    vl = [slice(hh * dv, (hh + 1) * dv) for hh in range(nh)]

    for gi in range(n_chunks // GLA_GROUP):
        r = slice(gi * grp, (gi + 1) * grp)
        for hh in range(nh):
            attn = jnp.where(causal, _dot_nt(qe_ref[0, r, kl[hh]], kn_ref[0, r, kl[hh]]), 0.0).astype(BF16)
            oi_ref[hh, r, :] = _dot(attn, v_ref[0, r, vl[hh]])
            for i in range(gi * GLA_GROUP, (gi + 1) * GLA_GROUP):
                rc = slice(i * c, (i + 1) * c)
                u_ref[hh, i] = _dot_tn(kd_ref[0, rc, kl[hh]], v_ref[0, rc, vl[hh]])

    gn = gn_ref[...]
    decay_t = dl_ref[0].T
    states = [jnp.zeros((dk, dv), F32) for _ in range(nh)]
    for i in range(n_chunks):
        rc = slice(i * c, (i + 1) * c)
        for hh in range(nh):
            o = oi_ref[hh, rc, :] + _dot(qe_ref[0, rc, kl[hh]], states[hh].astype(BF16))
            y = _rms(o, gn) * _silu(g_ref[0, rc, vl[hh]].astype(F32))
            o_ref[0, rc, vl[hh]] = y.astype(BF16)
            states[hh] = decay_t[kl[hh], i:i + 1] * states[hh] + u_ref[hh, i]


def _gla_mix(qe, kn, kd, v, g, dlast, gn):
    b, s, key_dim = qe.shape
    val_dim = v.shape[2]
    dk, dv = key_dim // GLA_HEADS, val_dim // GLA_HEADS
    n_chunks = s // GLA_CHUNK
    nh = GLA_STEP_HEADS
    kspec = pl.BlockSpec((1, s, nh * dk), lambda i, j: (i, 0, j))
    vspec = pl.BlockSpec((1, s, nh * dv), lambda i, j: (i, 0, j))
    return pl.pallas_call(
        _gla_mix_body,
        grid=(b, GLA_HEADS // nh),
        in_specs=[kspec, kspec, kspec, vspec, vspec,
                  pl.BlockSpec((1, n_chunks, nh * dk), lambda i, j: (i, 0, j)), _full((1, dv))],
        out_specs=vspec,
        out_shape=jax.ShapeDtypeStruct((b, s, val_dim), BF16),
        scratch_shapes=[pltpu.VMEM((nh, s, dv), F32), pltpu.VMEM((nh, n_chunks, dk, dv), F32)],
        compiler_params=_params("parallel", "parallel"),
        name="gla_mix",
    )(qe, kn, kd, v, g, dlast, gn.reshape(1, dv))


def _rope(x, cos, sin_signed):
    return x * cos + pltpu.roll(x, 64, 1) * sin_signed


def _mla_in_body(*refs, scale, fused):
    x, refs = _layer_input(refs, fused)
    (ng_ref, wa_ref, wg_ref, gq_ref, gkv_ref, wqn_ref, wqr_ref, wkn_ref, wv_ref,
     cos_ref, sin_ref, qn_ref, qr_ref, kn_ref, kr_ref, v_ref, g_ref) = refs
    h = _rms(x, ng_ref[...]).astype(BF16)
    pa = _dot(h, wa_ref[...])
    g_ref[...] = _dot(h, wg_ref[...]).astype(BF16)
    cos, sin = cos_ref[...], sin_ref[...]
    c_q = _rms(pa[:, :MLA_Q_RANK], gq_ref[...]).astype(BF16)
    c_kv = _rms(pa[:, MLA_Q_RANK:MLA_Q_RANK + MLA_KV_RANK], gkv_ref[...]).astype(BF16)
    kr_ref[...] = _rope(pa[:, MLA_Q_RANK + MLA_KV_RANK:], cos, sin).astype(BF16)
    qn_ref[...] = (_dot(c_q, wqn_ref[...]) * scale).astype(BF16)
    qr = _dot(c_q, wqr_ref[...])
    for p in range(qr.shape[1] // LANES):
        sl = slice(p * LANES, (p + 1) * LANES)
        qr_ref[:, sl] = (_rope(qr[:, sl], cos, sin) * scale).astype(BF16)
    kn_ref[...] = _dot(c_kv, wkn_ref[...]).astype(BF16)
    v_ref[...] = _dot(c_kv, wv_ref[...]).astype(BF16)


def _mla_in(x2, prev, ng, w_in, gq, gkv, w_uq, w_ukv, seq):
    t, d = x2.shape
    tm = ROW_TILE
    hq, half = MLA_HEADS, MLA_ROPE // 2
    o_kv, o_kr, o_g = MLA_Q_RANK, MLA_Q_RANK + MLA_KV_RANK, MLA_Q_RANK + MLA_KV_RANK + MLA_ROPE
    kr1, kr2 = w_in[:, o_kr:o_kr + half], w_in[:, o_kr + half:o_g]
    wa = jnp.concatenate([w_in[:, :o_kr], kr1, kr1, kr2, kr2], axis=1).astype(BF16)
    wg = w_in[:, o_g:].astype(BF16)
    uq = w_uq.reshape(MLA_Q_RANK, hq, MLA_NOPE + MLA_ROPE)
    wqn = uq[:, :, :MLA_NOPE].reshape(MLA_Q_RANK, hq * MLA_NOPE).astype(BF16)
    x1 = uq[:, :, MLA_NOPE:MLA_NOPE + half].reshape(MLA_Q_RANK, hq // 2, 2 * half)
    x2r = uq[:, :, MLA_NOPE + half:].reshape(MLA_Q_RANK, hq // 2, 2 * half)
    wqr = jnp.concatenate([x1, x2r], axis=2).reshape(MLA_Q_RANK, hq * MLA_ROPE).astype(BF16)
    ukv = w_ukv.reshape(MLA_KV_RANK, hq, MLA_NOPE + MLA_V)
    wkn = ukv[:, :, :MLA_NOPE].reshape(MLA_KV_RANK, hq * MLA_NOPE).astype(BF16)
    wv = ukv[:, :, MLA_NOPE:].reshape(MLA_KV_RANK, hq * MLA_V).astype(BF16)

    pos = jnp.arange(seq, dtype=F32)
    inv = ROPE_THETA ** (-jnp.arange(0, MLA_ROPE, 2, dtype=F32) / MLA_ROPE)
    ang = pos[:, None] * inv[None, :]
    cos, sin = jnp.cos(ang), jnp.sin(ang)
    cos4 = jnp.concatenate([cos, cos, cos, cos], axis=1)
    sin4 = jnp.concatenate([-sin, -sin, sin, sin], axis=1)

    scale = (MLA_NOPE + MLA_ROPE) ** -0.5
    tab = pl.BlockSpec((tm, LANES), lambda i: (i % (seq // tm), 0))
    wide = hq * MLA_NOPE
    arrays = [ng.reshape(1, d), wa, wg, gq.reshape(1, -1), gkv.reshape(1, -1), wqn, wqr, wkn, wv]
    widths = [wide, hq * MLA_ROPE, wide, LANES, wide, wide]
    return _in_call(
        functools.partial(_mla_in_body, scale=scale), "mla_in", x2, prev, arrays + [cos4, sin4],
        [_full(a.shape) for a in arrays] + [tab, tab],
        [_row(n) for n in widths], [jax.ShapeDtypeStruct((t, n), BF16) for n in widths])


def _mla_attn_body(qn_ref, qr_ref, kn_ref, kr_ref, v_ref, g_ref, o_ref, qc_ref, kc_ref, v1_ref):
    s = qn_ref.shape[1]
    t = MLA_TILE
    n_tiles = s // t
    lane = lax.broadcasted_iota(jnp.int32, (s, LANES), 1)
    for hh in range(MLA_STEP_HEADS):
        hl = slice(hh * LANES, (hh + 1) * LANES)
        mine = ((lane // (MLA_ROPE // 2)) % 2) == hh % 2
        rope = qr_ref[0, :, (hh // 2) * LANES:(hh // 2 + 1) * LANES]
        qc_ref[hh, :, :LANES] = qn_ref[0, :, hl]
        qc_ref[hh, :, LANES:] = jnp.where(mine, rope, jnp.zeros((), BF16))
        kc_ref[hh, :, :LANES] = kn_ref[0, :, hl]
        kc_ref[hh, :, LANES:] = kr_ref[0]
        v1_ref[hh, :, :LANES] = v_ref[0, :, hl]
        v1_ref[hh, :, LANES:] = jnp.ones((s, LANES), BF16)
    ri = lax.broadcasted_iota(jnp.int32, (t, t), 0)
    ci = lax.broadcasted_iota(jnp.int32, (t, t), 1)
    causal = ri >= ci

    for i in range(n_tiles):
        rq = slice(i * t, (i + 1) * t)
        for hh in range(MLA_STEP_HEADS):
            hl = slice(hh * LANES, (hh + 1) * LANES)
            q = qc_ref[hh, rq, :]
            s_diag = jnp.where(causal, _dot_nt(q, kc_ref[hh, rq, :]), -jnp.inf)
            m = jnp.max(s_diag, axis=-1, keepdims=True)
            if i > 0:
                s_past = _dot_nt(q, kc_ref[hh, :i * t, :])
                m = jnp.maximum(m, jnp.max(s_past, axis=-1, keepdims=True))
                acc = _dot(jnp.exp(s_past - m).astype(BF16), v1_ref[hh, :i * t, :])
            else:
                acc = jnp.zeros((t, 2 * LANES), F32)
            acc = acc + _dot(jnp.exp(s_diag - m).astype(BF16), v1_ref[hh, rq, :])
            o = acc[:, :LANES] / acc[:, LANES:]
            o_ref[0, rq, hl] = (o * _silu(g_ref[0, rq, hl].astype(F32))).astype(BF16)


def _mla_attn(qn, qr, kn, kr, v, g):
    b, s, wide = qn.shape
    nh = MLA_STEP_HEADS
    heads = pl.BlockSpec((1, s, nh * LANES), lambda i, j: (i, 0, j))
    return pl.pallas_call(
        _mla_attn_body,
        grid=(b, MLA_HEADS // nh),
        in_specs=[heads, pl.BlockSpec((1, s, nh // 2 * LANES), lambda i, j: (i, 0, j)), heads,
                  pl.BlockSpec((1, s, LANES), lambda i, j: (i, 0, 0)), heads, heads],
        out_specs=heads,
        out_shape=jax.ShapeDtypeStruct((b, s, wide), BF16),
        scratch_shapes=[pltpu.VMEM((nh, s, 2 * LANES), BF16)] * 3,
        compiler_params=_params("parallel", "parallel"),
        name="mla_attn",
    )(qn, qr, kn, kr, v, g)


def _sb_in_body(*refs, scale, fused):
    x, refs = _layer_input(refs, fused)
    ng_ref, wq_ref, wk_ref, wv_ref, wg_ref, q_ref, k_ref, v_ref, g_ref = refs
    h = _rms(x, ng_ref[...]).astype(BF16)
    q_ref[...] = (_dot(h, wq_ref[...]) * scale).astype(BF16)
    k_ref[...] = _dot(h, wk_ref[...]).astype(BF16)
    v_ref[...] = _dot(h, wv_ref[...]).astype(BF16)
    g_ref[...] = _dot(h, wg_ref[...]).astype(BF16)


def _sb_in(x2, prev, ng, w_in):
    t, d = x2.shape
    width = w_in.shape[1] // 4
    arrays = [ng.reshape(1, d)] + [w_in[:, i * width:(i + 1) * width].astype(BF16) for i in range(4)]
    return _in_call(
        functools.partial(_sb_in_body, scale=SB_HEAD_DIM ** -0.5), "sb_in", x2, prev, arrays,
        [_full(a.shape) for a in arrays], [_row(width)] * 4, [jax.ShapeDtypeStruct((t, width), BF16)] * 4)


def _sb_attn_body(q_ref, k_ref, v_ref, g_ref, m2_ref, o_ref, kp_ref, vp_ref):
    s = q_ref.shape[1]
    t = SB_TILE
    n_tiles = s // t
    ri = lax.broadcasted_iota(jnp.int32, (t, t), 0)
    ci = lax.broadcasted_iota(jnp.int32, (t, t), 1)
    strict = ri > ci
    lane = lax.broadcasted_iota(jnp.int32, (t, LANES), 1)
    first = lane < SB_HEAD_DIM
    m2 = m2_ref[...]

    def stats(z, masked):
        log_beta = _log_sigmoid(z)
        log_1m = log_beta - z
        if masked:
            log_1m = jnp.where(strict, log_1m, 0.0)
        hi = pltpu.bitcast(pltpu.bitcast(log_1m, jnp.uint32) & jnp.uint32(0xFFFF0000), F32)
        lo = (log_1m - hi).astype(BF16)
        res = _dot(jnp.concatenate([hi.astype(BF16), lo], axis=1), m2)
        return log_beta, res[:, :t], res[:, t:]

    def stats_pre(z, masked):
        log_beta = _log_sigmoid(z)
        log_1m = log_beta - z
        if masked:
            log_1m = jnp.where(strict, log_1m, 0.0)
        hi = pltpu.bitcast(pltpu.bitcast(log_1m, jnp.uint32) & jnp.uint32(0xFFFF0000), F32)
        lo = (log_1m - hi).astype(BF16)
        return log_beta, jnp.concatenate([hi.astype(BF16), lo], axis=1)

    def weights(log_beta, after, carry, masked):
        a = jnp.exp(log_beta + after + carry)
        if masked:
            a = jnp.where(strict, a, 0.0)
        return a.astype(BF16)

    pad = (SB_BAND - 1) * t
    kp_ref[:pad, :] = jnp.zeros((pad, LANES), BF16)
    vp_ref[:pad, :] = jnp.zeros((pad, LANES), BF16)
    kp_ref[pad:, :] = k_ref[0]
    vp_ref[pad:, :] = v_ref[0]

    def band(q2, i):
        rk = pl.ds(pl.multiple_of(i * t, t), SB_BAND * t)
        k, v = kp_ref[rk, :], vp_ref[rk, :]
        z2 = _dot_nt(q2, k)
        out = []
        pre = [[stats_pre(z2[hh * t:(hh + 1) * t, u * t:(u + 1) * t], u == SB_BAND - 1)
                for u in range(SB_BAND)] for hh in range(2)]
        res_all = _dot(jnp.concatenate([p[1] for row in pre for p in row], axis=0), m2)
        for hh in range(2):
            parts = []
            for u in range(SB_BAND):
                res = res_all[(hh * SB_BAND + u) * t:(hh * SB_BAND + u + 1) * t]
                parts.append((pre[hh][u][0], res[:, :t], res[:, t:]))
            carry = jnp.zeros((t, t), F32)
            a_parts = [None] * SB_BAND
            for u in reversed(range(SB_BAND)):
                log_beta, after, row_sum = parts[u]
                a_parts[u] = weights(log_beta, after, carry, u == SB_BAND - 1)
                carry = carry + row_sum
            out += [carry, _dot(jnp.concatenate(a_parts, axis=1), v)]
        return tuple(out)

    def tail(q2, j0, st):
        def cond(c):
            return (c[0] >= 0) & (jnp.max(jnp.maximum(c[1], c[3])) > -SB_CUTOFF)

        def body(c):
            j = c[0]
            rk = pl.ds(pl.multiple_of(j * t + pad, t), t)
            k, v = kp_ref[rk, :], vp_ref[rk, :]
            z2 = _dot_nt(q2, k)
            new = []
            for hh in range(2):
                carry, acc = c[1 + 2 * hh], c[2 + 2 * hh]
                log_beta, after, row_sum = stats(z2[hh * t:(hh + 1) * t], False)
                new += [carry + row_sum, acc + _dot(weights(log_beta, after, carry, False), v)]
            return (j - 1, *new)

        return lax.while_loop(cond, body, (j0, *st))[1:]

    def masked_queries(rq):
        q = q_ref[0, rq, :]
        zero = jnp.zeros((), BF16)
        return jnp.concatenate([jnp.where(first, q, zero), jnp.where(first, zero, q)], axis=0)

    def finish(rq, st):
        o = jnp.where(first, st[1], st[3])
        o_ref[0, rq, :] = (o * _silu(g_ref[0, rq, :].astype(F32))).astype(BF16)

    def group(gi, c0):
        tiles = []
        worst = None
        for u in range(SB_GROUP):
            i = gi * SB_GROUP + u
            rq = pl.ds(pl.multiple_of(i * t, t), t)
            q2 = masked_queries(rq)
            st = band(q2, i)
            finish(rq, st)
            tiles.append((i, rq, q2, st))
            c = jnp.maximum(st[0], st[2])
            worst = c if worst is None else jnp.maximum(worst, c)

        @pl.when(jnp.max(worst) > -SB_CUTOFF)
        def _():
            for i, rq, q2, st in tiles:
                finish(rq, tail(q2, i - SB_BAND, st))

        return c0

    lax.fori_loop(0, n_tiles // SB_GROUP, group, 0)


def _sb_attn(q, k, v, g):
    b, s, width = q.shape
    t = SB_TILE
    assert s % (SB_GROUP * t) == 0, s
    r = jnp.arange(2 * t)
    m2 = (((r[:, None] % t) > r[None, :]) | (r[None, :] >= t)).astype(BF16)
    spec = pl.BlockSpec((1, s, LANES), lambda i, j: (i, 0, j))
    return pl.pallas_call(
        _sb_attn_body,
        grid=(b, width // LANES),
        in_specs=[spec, spec, spec, spec, _full((2 * t, 2 * t))],
        out_specs=spec,
        out_shape=jax.ShapeDtypeStruct((b, s, width), BF16),
        scratch_shapes=[pltpu.VMEM((s + (SB_BAND - 1) * t, LANES), BF16)] * 2,
        compiler_params=_params("parallel", "parallel"),
        name="sb_attn",
    )(q, k, v, g, m2)


def _out_body(x_ref, a_ref, w_ref, fg_ref, o_ref):
    o_ref[...] = _rms(x_ref[...] + _dot(a_ref[...], w_ref[...]), fg_ref[...])


def _out_proj(x2, a2, w_out, final_g):
    t, d = x2.shape
    kdim = a2.shape[1]
    tm = 2 * ROW_TILE
    row = lambda n: pl.BlockSpec((tm, n), lambda i: (i, 0))
    return pl.pallas_call(
        _out_body,
        grid=(t // tm,),
        in_specs=[row(d), row(kdim), _full((kdim, d)), _full((1, d))],
        out_specs=row(d),
        out_shape=jax.ShapeDtypeStruct((t, d), F32),
        compiler_params=_params("parallel"),
        name="out_proj",
    )(x2, a2, w_out.astype(BF16), final_g.reshape(1, d))


def kernel(x, norm_g, final_g, gla_w_in, gla_w_gate2, gla_b_gate, gla_norm_g, gla_w_out,
           mla_w_in, mla_q_norm_g, mla_kv_norm_g, mla_w_uq, mla_w_ukv, mla_w_out,
           sb_w_in, sb_w_out):
    b, s, d = x.shape
    depth = norm_g.shape[0]
    assert s % max(2 * ROW_TILE, MLA_TILE, SB_TILE, GLA_CHUNK) == 0, (b, s, d)
    t = b * s
    x2 = x.reshape(t, d)
    sh = lambda a: a.reshape(b, s, a.shape[-1])
    prev = None
    for layer in range(depth):
        kind, j = layer % 3, layer // 3
        if kind == 0:
            (qe, kn, kd, v, g, dl), x2 = _gla_in(x2, prev, norm_g[layer], gla_w_in[j], gla_w_gate2[j],
                                                 gla_b_gate[j])
            y = _gla_mix(sh(qe), sh(kn), sh(kd), sh(v), sh(g), dl.reshape(b, s // GLA_CHUNK, -1), gla_norm_g[j])
            w_out = gla_w_out[j]
        elif kind == 1:
            parts, x2 = _mla_in(x2, prev, norm_g[layer], mla_w_in[j], mla_q_norm_g[j], mla_kv_norm_g[j],
                                mla_w_uq[j], mla_w_ukv[j], s)
            y = _mla_attn(*[sh(a) for a in parts])
            w_out = mla_w_out[j]
        else:
            parts, x2 = _sb_in(x2, prev, norm_g[layer], sb_w_in[j])
            y = _sb_attn(*[sh(a) for a in parts])
            w_out = sb_w_out[j]
        prev = (y.reshape(t, -1), w_out)
    return _out_proj(x2, prev[0], prev[1], final_g).reshape(b, s, d)
```
